```python
import math
import jax, jax.numpy as jnp
from jax import lax
import numpy as np

D_MODEL = 1024
BATCH = 8
SEQ = 8192
DEPTH = 1

CHUNK = 64
Q_BLOCK = 128
D_MIX = D_MODEL
ATT_WIDTH = D_MIX // 2
POOL_WIDTH = D_MIX - ATT_WIDTH
N_DIFF_HEADS = 4
DIFF_HEAD_DIM = ATT_WIDTH // N_DIFF_HEADS // 2
DIFF_V_DIM = 2 * DIFF_HEAD_DIM
POOL_WINDOWS = (2, 4, 8, 16)
N_POOL_GROUPS = len(POOL_WINDOWS)
POOL_GROUP_DIM = POOL_WIDTH // N_POOL_GROUPS
D_FF = ((8 * D_MODEL // 3 + 255) // 256) * 256
ROPE_THETA = 10000.0
NORM_EPS = 1e-6
LAMBDA_INIT_SCALE = 0.1
IN_WIDTH = 3 * ATT_WIDTH + POOL_WIDTH

kernel_name = "hybrid_diffattn_pool_macaron"


def rms_norm(x, g):
    xf = x.astype(jnp.float32)
    y = xf * lax.rsqrt(jnp.mean(xf * xf, axis=-1, keepdims=True) + NORM_EPS)
    return (y * g.astype(jnp.float32)).astype(x.dtype)


def swiglu(h, w_gate, w_up, w_down):
    return (jax.nn.silu(h @ w_gate) * (h @ w_up)) @ w_down


def rope_tables(seq, dim):
    pos = jnp.arange(seq, dtype=jnp.float32)
    inv_freq = 1.0 / (ROPE_THETA ** (jnp.arange(0, dim, 2, dtype=jnp.float32) / dim))
    ang = pos[:, None] * inv_freq[None, :]
    return jnp.cos(ang), jnp.sin(ang)


def apply_rope(t, cos, sin):
    tf = t.astype(jnp.float32)
    t1, t2 = jnp.split(tf, 2, axis=-1)
    c = cos[:, None, None, :]
    s = sin[:, None, None, :]
    out = jnp.concatenate([t1 * c - t2 * s, t1 * s + t2 * c], axis=-1)
    return out.astype(t.dtype)


def diff_attention(q, k, v, lam):
    B, S, H, _, Dh = q.shape
    nb = S // Q_BLOCK
    scale = DIFF_HEAD_DIM ** -0.5
    qb = (q * scale).reshape(B, nb, Q_BLOCK, H, 2, Dh).transpose(1, 4, 0, 3, 2, 5)
    kt = k.transpose(3, 0, 2, 1, 4)
    vt = v.transpose(0, 2, 1, 3)
    key_chunk = jnp.arange(S) // CHUNK

    def block(args):
        q_blk, bi = args
        q_chunk = (bi * Q_BLOCK + jnp.arange(Q_BLOCK)) // CHUNK
        mask = key_chunk[None, :] <= q_chunk[:, None]
        s = jnp.einsum('mbhqd,mbhkd->mbhqk', q_blk, kt).astype(jnp.float32)
        s = jnp.where(mask, s, -jnp.inf)
        p = jax.nn.softmax(s, axis=-1)
        a = p[0] - lam * p[1]
        return jnp.einsum('bhqk,bhkd->bhqd', a.astype(vt.dtype), vt)

    out = lax.map(block, (qb, jnp.arange(nb)))
    return out.transpose(1, 0, 3, 2, 4).reshape(B, S, H, DIFF_V_DIM)


def pool_mixer(u, w, scale):
    B, S, _ = u.shape
    ug = u.reshape(B, S, N_POOL_GROUPS, POOL_GROUP_DIM)
    ugf = ug.astype(jnp.float32)
    cs = jnp.cumsum(ugf, axis=1)
    t = jnp.arange(S)
    means = []
    for gi, win in enumerate(POOL_WINDOWS):
        c = cs[:, :, gi]
        lag = jnp.pad(c, ((0, 0), (win, 0), (0, 0)))[:, :S]
        cnt = jnp.minimum(t + 1, win).astype(jnp.float32)[None, :, None]
        means.append((c - lag) / cnt)
    d = (jnp.stack(means, axis=2) - ugf).astype(u.dtype)
    y = jnp.einsum('bsgc,gcd->bsgd', d, w)
    return y.reshape(B, S, POOL_WIDTH) * scale


def setup_inputs(seed: int = 0) -> dict:
    key = jax.random.key(seed)
    ks = jax.random.split(key, 24)
    f32 = jnp.float32

    def normal(k, shape, fan_in):
        return jax.random.normal(k, shape, f32) * (fan_in ** -0.5)

    def gain(k, shape):
        return 1.0 + 0.02 * jax.random.normal(k, shape, f32)

    return {
        "x": jax.random.normal(ks[0], (BATCH, SEQ, D_MODEL), f32),
        "ffn1_norm": gain(ks[1], (DEPTH, D_MODEL)),
        "ffn1_w_gate": normal(ks[2], (DEPTH, D_MODEL, D_FF), D_MODEL),
        "ffn1_w_up": normal(ks[3], (DEPTH, D_MODEL, D_FF), D_MODEL),
        "ffn1_w_down": normal(ks[4], (DEPTH, D_FF, D_MODEL), D_FF),
        "mix_norm": gain(ks[5], (DEPTH, D_MODEL)),
        "w_in": normal(ks[6], (DEPTH, D_MODEL, IN_WIDTH), D_MODEL),
        "lambda_q1": LAMBDA_INIT_SCALE * jax.random.normal(ks[7], (DEPTH, DIFF_HEAD_DIM), f32),
        "lambda_k1": LAMBDA_INIT_SCALE * jax.random.normal(ks[8], (DEPTH, DIFF_HEAD_DIM), f32),
        "lambda_q2": LAMBDA_INIT_SCALE * jax.random.normal(ks[9], (DEPTH, DIFF_HEAD_DIM), f32),
        "lambda_k2": LAMBDA_INIT_SCALE * jax.random.normal(ks[10], (DEPTH, DIFF_HEAD_DIM), f32),
        "subln_gain": gain(ks[11], (DEPTH, DIFF_V_DIM)),
        "pool_w": normal(ks[12], (DEPTH, N_POOL_GROUPS, POOL_GROUP_DIM, POOL_GROUP_DIM), POOL_GROUP_DIM),
        "pool_scale": 1.0 + 0.1 * jax.random.normal(ks[13], (DEPTH, POOL_WIDTH), f32),
        "w_out": normal(ks[14], (DEPTH, D_MIX, D_MODEL), D_MIX),
        "ffn2_norm": gain(ks[15], (DEPTH, D_MODEL)),
        "ffn2_w_gate": normal(ks[16], (DEPTH, D_MODEL, D_FF), D_MODEL),
        "ffn2_w_up": normal(ks[17], (DEPTH, D_MODEL, D_FF), D_MODEL),
        "ffn2_w_down": normal(ks[18], (DEPTH, D_FF, D_MODEL), D_FF),
        "final_norm": gain(ks[19], (D_MODEL,)),
    }


def reference(x, ffn1_norm, ffn1_w_gate, ffn1_w_up, ffn1_w_down, mix_norm, w_in,
              lambda_q1, lambda_k1, lambda_q2, lambda_k2, subln_gain, pool_w, pool_scale,
              w_out, ffn2_norm, ffn2_w_gate, ffn2_w_up, ffn2_w_down, final_norm):
    B, S, _ = x.shape
    cos, sin = rope_tables(S, DIFF_HEAD_DIM)
    for l in range(DEPTH):
        x = x + 0.5 * swiglu(rms_norm(x, ffn1_norm[l]), ffn1_w_gate[l], ffn1_w_up[l], ffn1_w_down[l])

        h = rms_norm(x, mix_norm[l])
        proj = h @ w_in[l]
        q = proj[..., :ATT_WIDTH].reshape(B, S, N_DIFF_HEADS, 2, DIFF_HEAD_DIM)
        k = proj[..., ATT_WIDTH:2 * ATT_WIDTH].reshape(B, S, N_DIFF_HEADS, 2, DIFF_HEAD_DIM)
        v = proj[..., 2 * ATT_WIDTH:3 * ATT_WIDTH].reshape(B, S, N_DIFF_HEADS, DIFF_V_DIM)
        u = proj[..., 3 * ATT_WIDTH:]

        q = apply_rope(q, cos, sin)
        k = apply_rope(k, cos, sin)
        lam_init = 0.8 - 0.6 * math.exp(-0.3 * l)
        lam = (jnp.exp(jnp.sum(lambda_q1[l].astype(jnp.float32) * lambda_k1[l].astype(jnp.float32)))
               - jnp.exp(jnp.sum(lambda_q2[l].astype(jnp.float32) * lambda_k2[l].astype(jnp.float32)))
               + lam_init)
        att = diff_attention(q, k, v, lam)
        att = (rms_norm(att, subln_gain[l]) * (1.0 - lam_init)).reshape(B, S, ATT_WIDTH)

        pool = pool_mixer(u, pool_w[l], pool_scale[l])

        x = x + jnp.concatenate([att.astype(x.dtype), pool.astype(x.dtype)], axis=-1) @ w_out[l]

        x = x + 0.5 * swiglu(rms_norm(x, ffn2_norm[l]), ffn2_w_gate[l], ffn2_w_up[l], ffn2_w_down[l])
    return rms_norm(x, final_norm)
```

```python
import functools
import math

import jax
import jax.numpy as jnp
from jax import lax
from jax.experimental import pallas as pl
from jax.experimental.pallas import tpu as pltpu

D_MODEL = 1024
CHUNK = 64
ATT_WIDTH = 512
POOL_WIDTH = 512
N_DIFF_HEADS = 4
DIFF_HEAD_DIM = 64
DIFF_V_DIM = 128
POOL_WINDOWS = (2, 4, 8, 16)
POOL_GROUP_DIM = 128
D_FF = 2816
ROPE_THETA = 10000.0
NORM_EPS = 1e-6
IN_WIDTH = 2048
QKV_WIDTH = 3 * ATT_WIDTH
LAM_INIT = 0.8 - 0.6 * math.exp(-0.3 * 0)

LANES = 128
SUBLANES = 8
VMEM_LIMIT_BYTES = 56 * 1024 * 1024

FF_CHUNK = 256
N_FF_CHUNKS = D_FF // FF_CHUNK
TOKEN_TILE = 512
POOL_HISTORY = 16
ATT_BLOCK = 512

_NT_DIMS = (((1,), (1,)), ((), ()))


def _rms_norm(x, g):
    return x * lax.rsqrt(jnp.mean(x * x, axis=-1, keepdims=True) + NORM_EPS) * g


def _swiglu_into(acc_ref, xn, wg_ref, wu_ref, wd_ref):
    acc_ref[...] = jnp.zeros_like(acc_ref)

    def body(c, carry):
        g = jnp.dot(xn, wg_ref[c], preferred_element_type=jnp.float32)
        u = jnp.dot(xn, wu_ref[c], preferred_element_type=jnp.float32)
        a = (g * jax.nn.sigmoid(g) * u).astype(jnp.bfloat16)
        acc_ref[...] += jnp.dot(a, wd_ref[c], preferred_element_type=jnp.float32)
        return carry

    lax.fori_loop(0, N_FF_CHUNKS, body, 0)


def _rope_slab(t, cos, sin_signed):
    lane = lax.broadcasted_iota(jnp.int32, t.shape, 1)
    first_half = (lane % DIFF_HEAD_DIM) < (DIFF_HEAD_DIM // 2)
    partner = jnp.where(first_half,
                        pltpu.roll(t, LANES - DIFF_HEAD_DIM // 2, 1),
                        pltpu.roll(t, DIFF_HEAD_DIM // 2, 1))
    return t * cos + partner * sin_signed


def _pre_kernel(tiles_per_seq,
                x_ref, g1_ref, wg_ref, wu_ref, wd_ref, gmix_ref, win_ref, cos_ref, sin_ref,
                poolw_ref, pscale_ref,
                x1_ref, qkv_ref, pool_ref,
                acc_ref, ext_ref):
    tm = x_ref.shape[0]
    x = x_ref[...]
    xn = _rms_norm(x, g1_ref[...]).astype(jnp.bfloat16)
    _swiglu_into(acc_ref, xn, wg_ref, wu_ref, wd_ref)
    x1 = x + 0.5 * acc_ref[...]
    x1_ref[...] = x1

    h = _rms_norm(x1, gmix_ref[...]).astype(jnp.bfloat16)
    cos = cos_ref[...]
    sin = sin_ref[...]
    n_slabs = ATT_WIDTH // LANES
    for s in range(2 * n_slabs):
        cols = slice(s * LANES, (s + 1) * LANES)
        t = jnp.dot(h, win_ref[:, cols], preferred_element_type=jnp.float32)
        r = _rope_slab(t, cos, sin)
        if s < n_slabs:
            r = r * (DIFF_HEAD_DIM ** -0.5)
        qkv_ref[:, cols] = r.astype(jnp.bfloat16)
    vcols = slice(2 * ATT_WIDTH, 3 * ATT_WIDTH)
    qkv_ref[:, vcols] = jnp.dot(h, win_ref[:, vcols],
                                preferred_element_type=jnp.float32).astype(jnp.bfloat16)

    tile_in_seq = pl.program_id(0) % tiles_per_seq

    @pl.when(tile_in_seq == 0)
    def _():
        ext_ref[0:POOL_HISTORY, :] = jnp.zeros((POOL_HISTORY, POOL_WIDTH), jnp.float32)

    ext_ref[POOL_HISTORY:, :] = jnp.dot(h, win_ref[:, 3 * ATT_WIDTH:],
                                        preferred_element_type=jnp.float32)
    pos = tile_in_seq * tm + lax.broadcasted_iota(jnp.int32, (tm, 1), 0)
    for gi, win in enumerate(POOL_WINDOWS):
        cols = slice(gi * POOL_GROUP_DIM, (gi + 1) * POOL_GROUP_DIM)
        u = ext_ref[POOL_HISTORY:, cols]
        wsum = u
        for j in range(1, win):
            wsum = wsum + ext_ref[POOL_HISTORY - j:POOL_HISTORY - j + tm, cols]
        cnt = jnp.minimum(pos + 1, win).astype(jnp.float32)
        d = (wsum / cnt - u).astype(jnp.bfloat16)
        y = jnp.dot(d, poolw_ref[gi], preferred_element_type=jnp.float32)
        pool_ref[:, cols] = (y * pscale_ref[:, cols]).astype(jnp.bfloat16)
    ext_ref[0:POOL_HISTORY, :] = ext_ref[tm:tm + POOL_HISTORY, :]


def _attn_kernel(q_ref, k_ref, v_ref, lq1_ref, lk1_ref, lq2_ref, lk2_ref, gain_ref,
                 o_ref, m_ref, acc_ref):
    tq = q_ref.shape[1]
    tk = tq
    i = pl.program_id(2)

    q = q_ref[0]
    lane = lax.broadcasted_iota(jnp.int32, q.shape, 1)
    zero = jnp.zeros_like(q)
    qs = jnp.concatenate([jnp.where(lane < DIFF_HEAD_DIM, q, zero),
                          jnp.where(lane >= DIFF_HEAD_DIM, q, zero)], axis=0)

    m_ref[...] = jnp.full(m_ref.shape, -jnp.inf, jnp.float32)
    acc_ref[...] = jnp.zeros_like(acc_ref)
    ones = jnp.ones((tk, LANES), jnp.bfloat16)

    def step(j, masked):
        start = pl.multiple_of(j * tk, tk)
        k = k_ref[0, pl.ds(start, tk), :]
        v = v_ref[0, pl.ds(start, tk), :]
        s = lax.dot_general(qs, k, _NT_DIMS, preferred_element_type=jnp.float32)
        if masked:
            row_chunk = (lax.broadcasted_iota(jnp.int32, s.shape, 0) % tq) // CHUNK
            col_chunk = lax.broadcasted_iota(jnp.int32, s.shape, 1) // CHUNK
            s = jnp.where(col_chunk <= row_chunk, s, -jnp.inf)
        slabs = [s[:, c * LANES:(c + 1) * LANES] for c in range(tk // LANES)]
        mx = functools.reduce(jnp.maximum, slabs)
        m_old = m_ref[...]
        m_new = jnp.maximum(m_old, jnp.max(mx, axis=1, keepdims=True))
        alpha = jnp.exp(m_old - m_new)
        p = jnp.concatenate([jnp.exp(sl - m_new) for sl in slabs], axis=1).astype(jnp.bfloat16)
        vext = jnp.concatenate([v, ones], axis=1)
        pv = jnp.dot(p, vext, preferred_element_type=jnp.float32)
        acc_ref[...] = acc_ref[...] * jnp.concatenate([alpha, alpha], axis=1) + pv
        m_ref[...] = m_new

    def body(j, carry):
        step(j, masked=False)
        return carry

    lax.fori_loop(0, i, body, 0)
    step(i, masked=True)

    lam = (jnp.exp(jnp.sum(lq1_ref[...] * lk1_ref[...], axis=1, keepdims=True))
           - jnp.exp(jnp.sum(lq2_ref[...] * lk2_ref[...], axis=1, keepdims=True))
           + LAM_INIT)
    acc = acc_ref[...]
    a = acc[:, :DIFF_V_DIM] / acc[:, DIFF_V_DIM:]
    o = a[:tq] - lam * a[tq:]
    o = _rms_norm(o, gain_ref[...]) * (1.0 - LAM_INIT)
    o_ref[0] = o.astype(o_ref.dtype)


def _post_kernel(x1_ref, att_ref, pool_ref, wout_ref, g2_ref, wg_ref, wu_ref, wd_ref, gf_ref,
                 y_ref, acc_ref):
    x2 = (x1_ref[...]
          + jnp.dot(att_ref[...], wout_ref[0:ATT_WIDTH, :], preferred_element_type=jnp.float32)
          + jnp.dot(pool_ref[...], wout_ref[ATT_WIDTH:, :], preferred_element_type=jnp.float32))
    xn = _rms_norm(x2, g2_ref[...]).astype(jnp.bfloat16)
    _swiglu_into(acc_ref, xn, wg_ref, wu_ref, wd_ref)
    x3 = x2 + 0.5 * acc_ref[...]
    y_ref[...] = _rms_norm(x3, gf_ref[...])


def _resident(shape):
    zeros = (0,) * len(shape)
    return pl.BlockSpec(shape, lambda *_: zeros, pipeline_mode=pl.Buffered(1))


def _ffn_weights(w_gate, w_up, w_down):
    wg = w_gate.astype(jnp.bfloat16).reshape(D_MODEL, N_FF_CHUNKS, FF_CHUNK).transpose(1, 0, 2)
    wu = w_up.astype(jnp.bfloat16).reshape(D_MODEL, N_FF_CHUNKS, FF_CHUNK).transpose(1, 0, 2)
    wd = w_down.astype(jnp.bfloat16).reshape(N_FF_CHUNKS, FF_CHUNK, D_MODEL)
    return wg, wu, wd


def _rope_tables(seq):
    half = DIFF_HEAD_DIM // 2
    pos = jnp.arange(seq, dtype=jnp.float32)
    inv_freq = 1.0 / (ROPE_THETA ** (jnp.arange(0, DIFF_HEAD_DIM, 2, dtype=jnp.float32) / DIFF_HEAD_DIM))
    ang = pos[:, None] * inv_freq[None, :]
    cos, sin = jnp.cos(ang), jnp.sin(ang)
    reps = LANES // half
    cos_t = jnp.tile(cos, (1, reps))
    sin_t = jnp.tile(jnp.concatenate([-sin, sin], axis=1), (1, reps // 2))
    return cos_t, sin_t


def kernel(x, ffn1_norm, ffn1_w_gate, ffn1_w_up, ffn1_w_down, mix_norm, w_in, lambda_q1, lambda_k1, lambda_q2, lambda_k2, subln_gain, pool_w, pool_scale, w_out, ffn2_norm, ffn2_w_gate, ffn2_w_up, ffn2_w_down, final_norm):
    B, S, D = x.shape
    assert D == D_MODEL and ffn1_norm.shape[0] == 1
    tm = min(TOKEN_TILE, S)
    tq = min(ATT_BLOCK, S)
    assert S % tm == 0 and S % tq == 0 and tq % CHUNK == 0 and tm >= POOL_HISTORY
    T = B * S
    n_tiles = T // tm
    tiles_per_seq = S // tm
    f32, bf16 = jnp.float32, jnp.bfloat16

    xf = x.reshape(T, D)
    row = lambda a: a.reshape(1, -1).astype(f32)
    wg1, wu1, wd1 = _ffn_weights(ffn1_w_gate[0], ffn1_w_up[0], ffn1_w_down[0])
    wg2, wu2, wd2 = _ffn_weights(ffn2_w_gate[0], ffn2_w_up[0], ffn2_w_down[0])
    cos_t, sin_t = _rope_tables(S)
    params = pltpu.CompilerParams(dimension_semantics=("arbitrary",), vmem_limit_bytes=VMEM_LIMIT_BYTES)

    tok = lambda width: pl.BlockSpec((tm, width), lambda t: (t, 0))
    tab = pl.BlockSpec((tm, LANES), lambda t: (t % tiles_per_seq, 0))
    ffn_specs = [_resident((N_FF_CHUNKS, D, FF_CHUNK)), _resident((N_FF_CHUNKS, D, FF_CHUNK)),
                 _resident((N_FF_CHUNKS, FF_CHUNK, D))]

    x1, qkv, pool = pl.pallas_call(
        functools.partial(_pre_kernel, tiles_per_seq),
        grid=(n_tiles,),
        in_specs=[tok(D), _resident((1, D))] + ffn_specs
                 + [_resident((1, D)), _resident((D, IN_WIDTH)), tab, tab,
                    _resident((len(POOL_WINDOWS), POOL_GROUP_DIM, POOL_GROUP_DIM)),
                    _resident((1, POOL_WIDTH))],
        out_specs=[tok(D), tok(QKV_WIDTH), tok(POOL_WIDTH)],
        out_shape=[jax.ShapeDtypeStruct((T, D), f32),
                   jax.ShapeDtypeStruct((T, QKV_WIDTH), bf16),
                   jax.ShapeDtypeStruct((T, POOL_WIDTH), bf16)],
        scratch_shapes=[pltpu.VMEM((tm, D), f32),
                        pltpu.VMEM((POOL_HISTORY + tm, POOL_WIDTH), f32)],
        compiler_params=params,
        name="pre_ffn_proj_pool",
    )(xf, row(ffn1_norm[0]), wg1, wu1, wd1, row(mix_norm[0]), w_in[0].astype(bf16), cos_t, sin_t,
      pool_w[0].astype(bf16), row(pool_scale[0]))

    qkv3 = qkv.reshape(B, S, QKV_WIDTH)
    n_heads = N_DIFF_HEADS
    lam_spec = _resident((1, DIFF_HEAD_DIM))
    att = pl.pallas_call(
        _attn_kernel,
        grid=(B, n_heads, S // tq),
        in_specs=[pl.BlockSpec((1, tq, LANES), lambda b, h, i: (b, i, h)),
                  pl.BlockSpec((1, S, LANES), lambda b, h, i: (b, 0, n_heads + h)),
                  pl.BlockSpec((1, S, LANES), lambda b, h, i: (b, 0, 2 * n_heads + h)),
                  lam_spec, lam_spec, lam_spec, lam_spec, _resident((1, DIFF_V_DIM))],
        out_specs=pl.BlockSpec((1, tq, LANES), lambda b, h, i: (b, i, h)),
        out_shape=jax.ShapeDtypeStruct((B, S, ATT_WIDTH), bf16),
        scratch_shapes=[pltpu.VMEM((2 * tq, LANES), f32),
                        pltpu.VMEM((2 * tq, 2 * LANES), f32)],
        compiler_params=pltpu.CompilerParams(
            dimension_semantics=("arbitrary", "arbitrary", "arbitrary"),
            vmem_limit_bytes=VMEM_LIMIT_BYTES),
        name="diff_flash_attention",
    )(qkv3, qkv3, qkv3, row(lambda_q1[0]), row(lambda_k1[0]), row(lambda_q2[0]), row(lambda_k2[0]),
      row(subln_gain[0]))

    y = pl.pallas_call(
        _post_kernel,
        grid=(n_tiles,),
        in_specs=[tok(D), tok(ATT_WIDTH), tok(POOL_WIDTH), _resident((D, D)), _resident((1, D))]
                 + ffn_specs + [_resident((1, D))],
        out_specs=tok(D),
        out_shape=jax.ShapeDtypeStruct((T, D), f32),
        scratch_shapes=[pltpu.VMEM((tm, D), f32)],
        compiler_params=params,
        name="post_proj_ffn_norm",
    )(x1, att.reshape(T, ATT_WIDTH), pool, w_out[0].astype(bf16), row(ffn2_norm[0]),
      wg2, wu2, wd2, row(final_norm))
    return y.reshape(B, S, D)
```

```python
import functools
import math

import jax
import jax.numpy as jnp
from jax import lax
from jax.experimental import pallas as pl
from jax.experimental.pallas import tpu as pltpu

D_MODEL = 1024
CHUNK = 64
ATT_WIDTH = 512
POOL_WIDTH = 512
N_DIFF_HEADS = 4
DIFF_HEAD_DIM = 64
DIFF_V_DIM = 128
POOL_WINDOWS = (2, 4, 8, 16)
POOL_GROUP_DIM = 128
D_FF = 2816
ROPE_THETA = 10000.0
NORM_EPS = 1e-6
IN_WIDTH = 2048
LAM_INIT = 0.8 - 0.6 * math.exp(-0.3 * 0)
Q_SCALE = DIFF_HEAD_DIM ** -0.5 * math.log2(math.e)

LANES = 128
SUBLANES = 8
VMEM_LIMIT_BYTES = 56 * 1024 * 1024

FF_CHUNK = 256
N_FF_CHUNKS = D_FF // FF_CHUNK
TOKEN_TILE = 512
POOL_HISTORY = 16
ATT_BLOCK = TOKEN_TILE


def _rms_norm(x, g):
    return x * lax.rsqrt(jnp.mean(x * x, axis=-1, keepdims=True) + NORM_EPS) * g


def _swiglu_into(acc_ref, xn, wg_ref, wu_ref, wd_ref):
    acc_ref[...] = jnp.zeros_like(acc_ref)

    def body(c, carry):
        g = jnp.dot(xn, wg_ref[c], preferred_element_type=jnp.float32)
        u = jnp.dot(xn, wu_ref[c], preferred_element_type=jnp.float32)
        a = (g * jax.nn.sigmoid(g) * u).astype(jnp.bfloat16)
        acc_ref[...] += jnp.dot(a, wd_ref[c], preferred_element_type=jnp.float32)
        return carry

    lax.fori_loop(0, N_FF_CHUNKS, body, 0, unroll=True)


def _rope_slab(t, cos, sin_signed):
    lane = lax.broadcasted_iota(jnp.int32, t.shape, 1)
    first_half = (lane % DIFF_HEAD_DIM) < (DIFF_HEAD_DIM // 2)
    partner = jnp.where(first_half,
                        pltpu.roll(t, LANES - DIFF_HEAD_DIM // 2, 1),
                        pltpu.roll(t, DIFF_HEAD_DIM // 2, 1))
    return t * cos + partner * sin_signed


def _pre_kernel(tiles_per_seq,
                x_ref, g1_ref, wg_ref, wu_ref, wd_ref, gmix_ref, win_ref, cos_ref, sin_ref,
                poolw_ref, pscale_ref,
                x1_ref, q_ref, kt_ref, v_ref, pool_ref,
                acc_ref, qkv_ref, ext_ref):
    tm = x_ref.shape[0]
    x = x_ref[...]
    xn = _rms_norm(x, g1_ref[...]).astype(jnp.bfloat16)
    _swiglu_into(acc_ref, xn, wg_ref, wu_ref, wd_ref)
    x1 = x + 0.5 * acc_ref[...]
    x1_ref[...] = x1

    h = _rms_norm(x1, gmix_ref[...]).astype(jnp.bfloat16)
    cos = cos_ref[...]
    sin = sin_ref[...]
    qkv_ref[...] = jnp.dot(h, win_ref[:, :3 * ATT_WIDTH], preferred_element_type=jnp.float32)
    for hd in range(N_DIFF_HEADS):
        cols = slice(hd * LANES, (hd + 1) * LANES)
        q_ref[:, cols] = (_rope_slab(qkv_ref[:, cols], cos, sin) * Q_SCALE).astype(jnp.bfloat16)
        kcols = slice(ATT_WIDTH + hd * LANES, ATT_WIDTH + (hd + 1) * LANES)
        kt_ref[0, hd, 0] = _rope_slab(qkv_ref[:, kcols], cos, sin).T.astype(jnp.bfloat16)
    v_ref[...] = qkv_ref[:, 2 * ATT_WIDTH:].astype(jnp.bfloat16)

    tile_in_seq = pl.program_id(0) % tiles_per_seq

    @pl.when(tile_in_seq == 0)
    def _():
        ext_ref[0:POOL_HISTORY, :] = jnp.zeros((POOL_HISTORY, POOL_WIDTH), jnp.float32)

    ext_ref[POOL_HISTORY:, :] = jnp.dot(h, win_ref[:, 3 * ATT_WIDTH:],
                                        preferred_element_type=jnp.float32)
    pos = tile_in_seq * tm + lax.broadcasted_iota(jnp.int32, (tm, 1), 0)
    for gi, win in enumerate(POOL_WINDOWS):
        cols = slice(gi * POOL_GROUP_DIM, (gi + 1) * POOL_GROUP_DIM)
        ext = ext_ref[:, cols]
        wsum = ext
        shift = 1
        while shift < win:
            wsum = wsum + pltpu.roll(wsum, shift, 0)
            shift *= 2
        inv_cnt = 1.0 / jnp.minimum(pos + 1, win).astype(jnp.float32)
        d = (wsum[POOL_HISTORY:] * inv_cnt - ext[POOL_HISTORY:]).astype(jnp.bfloat16)
        y = jnp.dot(d, poolw_ref[gi], preferred_element_type=jnp.float32)
        pool_ref[:, cols] = (y * pscale_ref[:, cols]).astype(jnp.bfloat16)
    ext_ref[0:POOL_HISTORY, :] = ext_ref[tm:tm + POOL_HISTORY, :]


def _softmax_pv(s_ref, rows, m_ref, acc_ref, vext, masked):
    s = s_ref[rows, :]
    tk = s.shape[1]
    if masked:
        row_chunk = lax.broadcasted_iota(jnp.int32, s.shape, 0) // CHUNK
        col_chunk = lax.broadcasted_iota(jnp.int32, s.shape, 1) // CHUNK
        s = jnp.where(col_chunk <= row_chunk, s, -jnp.inf)
    slabs = [s[:, c * LANES:(c + 1) * LANES] for c in range(tk // LANES)]
    mx = functools.reduce(jnp.maximum, slabs)
    m_old = m_ref[rows, :]
    m_new = jnp.maximum(m_old, jnp.max(mx, axis=1, keepdims=True))
    alpha = jnp.exp2(m_old - m_new)
    p = jnp.concatenate([jnp.exp2(sl - m_new) for sl in slabs], axis=1).astype(jnp.bfloat16)
    pv = jnp.dot(p, vext, preferred_element_type=jnp.float32)
    acc_ref[rows, :] = acc_ref[rows, :] * jnp.concatenate([alpha, alpha], axis=1) + pv
    m_ref[rows, :] = m_new


def _attn_kernel(q_ref, kt_ref, v_ref, lq1_ref, lk1_ref, lq2_ref, lk2_ref, gain_ref,
                 o_ref, qz_ref, sa_ref, sb_ref, m_ref, acc_ref):
    tq = q_ref.shape[1]
    tk = kt_ref.shape[-1]
    i = pl.program_id(2)
    map_rows = (slice(0, tq), slice(tq, 2 * tq))

    q = q_ref[0]
    lane = lax.broadcasted_iota(jnp.int32, q.shape, 1)
    zero = jnp.zeros_like(q)
    qz_ref[map_rows[0], :] = jnp.where(lane < DIFF_HEAD_DIM, q, zero)
    qz_ref[map_rows[1], :] = jnp.where(lane >= DIFF_HEAD_DIM, q, zero)
    m_ref[...] = jnp.full(m_ref.shape, -jnp.inf, jnp.float32)
    acc_ref[...] = jnp.zeros_like(acc_ref)
    ones = jnp.ones((tk, LANES), jnp.bfloat16)

    def produce(j, s_ref):
        kt = kt_ref[0, 0, j]
        for rows in map_rows:
            s_ref[rows, :] = jnp.dot(qz_ref[rows, :], kt, preferred_element_type=jnp.float32)

    def consume(j, s_ref, masked):
        start = pl.multiple_of(j * tk, tk)
        vext = jnp.concatenate([v_ref[0, pl.ds(start, tk), :], ones], axis=1)
        for rows in map_rows:
            _softmax_pv(s_ref, rows, m_ref, acc_ref, vext, masked)

    produce(0, sa_ref)

    def pair(jj, carry):
        j = 2 * jj
        produce(j + 1, sb_ref)
        consume(j, sa_ref, masked=False)
        produce(j + 2, sa_ref)
        consume(j + 1, sb_ref, masked=False)
        return carry

    lax.fori_loop(0, i // 2, pair, 0)

    @pl.when(i % 2 == 0)
    def _():
        consume(i, sa_ref, masked=True)

    @pl.when(i % 2 == 1)
    def _():
        produce(i, sb_ref)
        consume(i - 1, sa_ref, masked=False)
        consume(i, sb_ref, masked=True)

    lam = (jnp.exp(jnp.sum(lq1_ref[...] * lk1_ref[...], axis=1, keepdims=True))
           - jnp.exp(jnp.sum(lq2_ref[...] * lk2_ref[...], axis=1, keepdims=True))
           + LAM_INIT)
    acc = acc_ref[...]
    a = acc[:, :DIFF_V_DIM] / acc[:, DIFF_V_DIM:]
    o = a[:tq] - lam * a[tq:]
    o = _rms_norm(o, gain_ref[...]) * (1.0 - LAM_INIT)
    o_ref[0] = o.astype(o_ref.dtype)


def _post_kernel(x1_ref, att_ref, pool_ref, wout_ref, g2_ref, wg_ref, wu_ref, wd_ref, gf_ref,
                 y_ref, acc_ref):
    x2 = (x1_ref[...]
          + jnp.dot(att_ref[...], wout_ref[0:ATT_WIDTH, :], preferred_element_type=jnp.float32)
          + jnp.dot(pool_ref[...], wout_ref[ATT_WIDTH:, :], preferred_element_type=jnp.float32))
    xn = _rms_norm(x2, g2_ref[...]).astype(jnp.bfloat16)
    _swiglu_into(acc_ref, xn, wg_ref, wu_ref, wd_ref)
    x3 = x2 + 0.5 * acc_ref[...]
    y_ref[...] = _rms_norm(x3, gf_ref[...])


def _resident(shape):
    zeros = (0,) * len(shape)
    return pl.BlockSpec(shape, lambda *_: zeros, pipeline_mode=pl.Buffered(1))


def _ffn_weights(w_gate, w_up, w_down):
    wg = w_gate.astype(jnp.bfloat16).reshape(D_MODEL, N_FF_CHUNKS, FF_CHUNK).transpose(1, 0, 2)
    wu = w_up.astype(jnp.bfloat16).reshape(D_MODEL, N_FF_CHUNKS, FF_CHUNK).transpose(1, 0, 2)
    wd = w_down.astype(jnp.bfloat16).reshape(N_FF_CHUNKS, FF_CHUNK, D_MODEL)
    return wg, wu, wd


def _rope_tables(seq):
    half = DIFF_HEAD_DIM // 2
    pos = jnp.arange(seq, dtype=jnp.float32)
    inv_freq = 1.0 / (ROPE_THETA ** (jnp.arange(0, DIFF_HEAD_DIM, 2, dtype=jnp.float32) / DIFF_HEAD_DIM))
    ang = pos[:, None] * inv_freq[None, :]
    cos, sin = jnp.cos(ang), jnp.sin(ang)
    reps = LANES // half
    cos_t = jnp.tile(cos, (1, reps))
    sin_t = jnp.tile(jnp.concatenate([-sin, sin], axis=1), (1, reps // 2))
    return cos_t, sin_t


def kernel(x, ffn1_norm, ffn1_w_gate, ffn1_w_up, ffn1_w_down, mix_norm, w_in, lambda_q1, lambda_k1, lambda_q2, lambda_k2, subln_gain, pool_w, pool_scale, w_out, ffn2_norm, ffn2_w_gate, ffn2_w_up, ffn2_w_down, final_norm):
    B, S, D = x.shape
    assert D == D_MODEL and ffn1_norm.shape[0] == 1
    tm = min(TOKEN_TILE, S)
    tq = tk = min(ATT_BLOCK, S)
    assert tm == tk and S % tm == 0 and tq % CHUNK == 0 and tm >= POOL_HISTORY
    T = B * S
    n_kblocks = S // tk
    n_heads = N_DIFF_HEADS
    n_tiles = T // tm
    tiles_per_seq = S // tm
    f32, bf16 = jnp.float32, jnp.bfloat16

    xf = x.reshape(T, D)
    row = lambda a: a.reshape(1, -1).astype(f32)
    wg1, wu1, wd1 = _ffn_weights(ffn1_w_gate[0], ffn1_w_up[0], ffn1_w_down[0])
    wg2, wu2, wd2 = _ffn_weights(ffn2_w_gate[0], ffn2_w_up[0], ffn2_w_down[0])
    cos_t, sin_t = _rope_tables(S)
    params = pltpu.CompilerParams(dimension_semantics=("arbitrary",), vmem_limit_bytes=VMEM_LIMIT_BYTES)

    tok = lambda width: pl.BlockSpec((tm, width), lambda t: (t, 0))
    tab = pl.BlockSpec((tm, LANES), lambda t: (t % tiles_per_seq, 0))
    ffn_specs = [_resident((N_FF_CHUNKS, D, FF_CHUNK)), _resident((N_FF_CHUNKS, D, FF_CHUNK)),
                 _resident((N_FF_CHUNKS, FF_CHUNK, D))]

    kt_spec = pl.BlockSpec((1, n_heads, 1, LANES, tk),
                           lambda t: (t // tiles_per_seq, 0, t % tiles_per_seq, 0, 0))
    x1, q, kt, v, pool = pl.pallas_call(
        functools.partial(_pre_kernel, tiles_per_seq),
        grid=(n_tiles,),
        in_specs=[tok(D), _resident((1, D))] + ffn_specs
                 + [_resident((1, D)), _resident((D, IN_WIDTH)), tab, tab,
                    _resident((len(POOL_WINDOWS), POOL_GROUP_DIM, POOL_GROUP_DIM)),
                    _resident((1, POOL_WIDTH))],
        out_specs=[tok(D), tok(ATT_WIDTH), kt_spec, tok(ATT_WIDTH), tok(POOL_WIDTH)],
        out_shape=[jax.ShapeDtypeStruct((T, D), f32),
                   jax.ShapeDtypeStruct((T, ATT_WIDTH), bf16),
                   jax.ShapeDtypeStruct((B, n_heads, n_kblocks, LANES, tk), bf16),
                   jax.ShapeDtypeStruct((T, ATT_WIDTH), bf16),
                   jax.ShapeDtypeStruct((T, POOL_WIDTH), bf16)],
        scratch_shapes=[pltpu.VMEM((tm, D), f32),
                        pltpu.VMEM((tm, 3 * ATT_WIDTH), f32),
                        pltpu.VMEM((POOL_HISTORY + tm, POOL_WIDTH), f32)],
        compiler_params=params,
        name="pre_ffn_proj_pool",
    )(xf, row(ffn1_norm[0]), wg1, wu1, wd1, row(mix_norm[0]), w_in[0].astype(bf16), cos_t, sin_t,
      pool_w[0].astype(bf16), row(pool_scale[0]))

    lam_spec = _resident((1, DIFF_HEAD_DIM))
    att = pl.pallas_call(
        _attn_kernel,
        grid=(B, n_heads, S // tq),
        in_specs=[pl.BlockSpec((1, tq, LANES), lambda b, h, i: (b, i, h)),
                  pl.BlockSpec((1, 1, n_kblocks, LANES, tk), lambda b, h, i: (b, h, 0, 0, 0)),
                  pl.BlockSpec((1, S, LANES), lambda b, h, i: (b, 0, h)),
                  lam_spec, lam_spec, lam_spec, lam_spec, _resident((1, DIFF_V_DIM))],
        out_specs=pl.BlockSpec((1, tq, LANES), lambda b, h, i: (b, i, h)),
        out_shape=jax.ShapeDtypeStruct((B, S, ATT_WIDTH), bf16),
        scratch_shapes=[pltpu.VMEM((2 * tq, LANES), bf16),
                        pltpu.VMEM((2 * tq, tk), f32),
                        pltpu.VMEM((2 * tq, tk), f32),
                        pltpu.VMEM((2 * tq, LANES), f32),
                        pltpu.VMEM((2 * tq, 2 * LANES), f32)],
        compiler_params=pltpu.CompilerParams(
            dimension_semantics=("arbitrary", "arbitrary", "arbitrary"),
            vmem_limit_bytes=VMEM_LIMIT_BYTES),
        name="diff_flash_attention",
    )(q.reshape(B, S, ATT_WIDTH), kt, v.reshape(B, S, ATT_WIDTH),
      row(lambda_q1[0]), row(lambda_k1[0]), row(lambda_q2[0]), row(lambda_k2[0]), row(subln_gain[0]))

    y = pl.pallas_call(
        _post_kernel,
        grid=(n_tiles,),
        in_specs=[tok(D), tok(ATT_WIDTH), tok(POOL_WIDTH), _resident((D, D)), _resident((1, D))]
                 + ffn_specs + [_resident((1, D))],
        out_specs=tok(D),
        out_shape=jax.ShapeDtypeStruct((T, D), f32),
        scratch_shapes=[pltpu.VMEM((tm, D), f32)],
        compiler_params=params,
        name="post_proj_ffn_norm",
    )(x1, att.reshape(T, ATT_WIDTH), pool, w_out[0].astype(bf16), row(ffn2_norm[0]),
      wg2, wu2, wd2, row(final_norm))
    return y.reshape(B, S, D)
```

```python
import functools
import math

import jax
import jax.numpy as jnp
import numpy as np
from jax import lax
from jax.experimental import pallas as pl
from jax.experimental.pallas import tpu as pltpu

D_MODEL = 1024
CHUNK = 64
ATT_WIDTH = 512
POOL_WIDTH = 512
N_DIFF_HEADS = 4
DIFF_HEAD_DIM = 64
DIFF_V_DIM = 128
POOL_WINDOWS = (2, 4, 8, 16)
POOL_GROUP_DIM = 128
D_FF = 2816
ROPE_THETA = 10000.0
NORM_EPS = 1e-6
IN_WIDTH = 2048
LAM_INIT = 0.8 - 0.6 * math.exp(-0.3 * 0)
Q_SCALE = DIFF_HEAD_DIM ** -0.5 * math.log2(math.e)

LANES = 128
SUBLANES = 8
VMEM_LIMIT_BYTES = 56 * 1024 * 1024

FF_CHUNK = 256
N_FF_CHUNKS = D_FF // FF_CHUNK
TOKEN_TILE = 512
POOL_HISTORY = 16
ATT_BLOCK = TOKEN_TILE
ATT_GROUPS = 2


def _rms_norm(x, g):
    return x * lax.rsqrt(jnp.mean(x * x, axis=-1, keepdims=True) + NORM_EPS) * g


def _swiglu_into(acc_ref, xn, wg_ref, wu_ref, wd_ref):
    acc_ref[...] = jnp.zeros_like(acc_ref)

    def body(c, carry):
        g = jnp.dot(xn, wg_ref[c], preferred_element_type=jnp.float32)
        u = jnp.dot(xn, wu_ref[c], preferred_element_type=jnp.float32)
        a = (g * jax.nn.sigmoid(g) * u).astype(jnp.bfloat16)
        acc_ref[...] += jnp.dot(a, wd_ref[c], preferred_element_type=jnp.float32)
        return carry

    lax.fori_loop(0, N_FF_CHUNKS, body, 0, unroll=True)


def _rope_slab(t, cos, sin_signed):
    lane = lax.broadcasted_iota(jnp.int32, t.shape, 1)
    first_half = (lane % DIFF_HEAD_DIM) < (DIFF_HEAD_DIM // 2)
    partner = jnp.where(first_half,
                        pltpu.roll(t, LANES - DIFF_HEAD_DIM // 2, 1),
                        pltpu.roll(t, DIFF_HEAD_DIM // 2, 1))
    return t * cos + partner * sin_signed


def _pre_kernel(tiles_per_seq,
                x_ref, g1_ref, wg_ref, wu_ref, wd_ref, gmix_ref, win_ref, cos_ref, sin_ref,
                poolw_ref, pscale_ref,
                x1_ref, qz_ref, kt_ref, v_ref, pool_ref,
                acc_ref, qkv_ref, ext_ref):
    tm = x_ref.shape[0]
    x = x_ref[...]
    xn = _rms_norm(x, g1_ref[...]).astype(jnp.bfloat16)
    _swiglu_into(acc_ref, xn, wg_ref, wu_ref, wd_ref)
    x1 = x + 0.5 * acc_ref[...]
    x1_ref[...] = x1

    h = _rms_norm(x1, gmix_ref[...]).astype(jnp.bfloat16)
    cos = cos_ref[...]
    sin = sin_ref[...]
    qkv_ref[...] = jnp.dot(h, win_ref[:, :3 * ATT_WIDTH], preferred_element_type=jnp.float32)
    lane = lax.broadcasted_iota(jnp.int32, (tm, LANES), 1)
    zero = jnp.zeros((tm, LANES), jnp.bfloat16)
    for hd in range(N_DIFF_HEADS):
        cols = slice(hd * LANES, (hd + 1) * LANES)
        q = (_rope_slab(qkv_ref[:, cols], cos, sin) * Q_SCALE).astype(jnp.bfloat16)
        qz_ref[0, :, cols] = jnp.where(lane < DIFF_HEAD_DIM, q, zero)
        qz_ref[1, :, cols] = jnp.where(lane >= DIFF_HEAD_DIM, q, zero)
        kcols = slice(ATT_WIDTH + hd * LANES, ATT_WIDTH + (hd + 1) * LANES)
        kt_ref[0, hd, 0] = _rope_slab(qkv_ref[:, kcols], cos, sin).T.astype(jnp.bfloat16)
    v_ref[...] = qkv_ref[:, 2 * ATT_WIDTH:].astype(jnp.bfloat16)

    tile_in_seq = pl.program_id(0) % tiles_per_seq

    @pl.when(tile_in_seq == 0)
    def _():
        ext_ref[0:POOL_HISTORY, :] = jnp.zeros((POOL_HISTORY, POOL_WIDTH), jnp.float32)

    ext_ref[POOL_HISTORY:, :] = jnp.dot(h, win_ref[:, 3 * ATT_WIDTH:],
                                        preferred_element_type=jnp.float32)
    pos = tile_in_seq * tm + lax.broadcasted_iota(jnp.int32, (tm, 1), 0)
    for gi, win in enumerate(POOL_WINDOWS):
        cols = slice(gi * POOL_GROUP_DIM, (gi + 1) * POOL_GROUP_DIM)
        ext = ext_ref[:, cols]
        wsum = ext
        shift = 1
        while shift < win:
            wsum = wsum + pltpu.roll(wsum, shift, 0)
            shift *= 2
        inv_cnt = 1.0 / jnp.minimum(pos + 1, win).astype(jnp.float32)
        d = (wsum[POOL_HISTORY:] * inv_cnt - ext[POOL_HISTORY:]).astype(jnp.bfloat16)
        y = jnp.dot(d, poolw_ref[gi], preferred_element_type=jnp.float32)
        pool_ref[:, cols] = (y * pscale_ref[:, cols]).astype(jnp.bfloat16)
    ext_ref[0:POOL_HISTORY, :] = ext_ref[tm:tm + POOL_HISTORY, :]


def _attn_schedule(n_qblocks, n_groups):
    gsz = n_qblocks // n_groups
    rows, n_pairs = [], []
    for g in range(n_groups):
        qblocks = range(g * gsz, (g + 1) * gsz)
        steps = [(i - g * gsz, 2 * c, 2 * c + 1) for i in qblocks for c in range(i // 2)]
        n_pairs.append(len(steps))
        steps += [(i - g * gsz, i - 1, i) for i in qblocks if i % 2 == 1]
        steps += [(i - g * gsz, i, min(i + 1, n_qblocks - 1)) for i in qblocks if i % 2 == 0]
        steps.append(steps[-1])
        rows.append(steps)
    n_tab = max(len(r) for r in rows)
    rows = [r + [r[-1]] * (n_tab - len(r)) for r in rows]
    column = lambda k: np.array([[st[k] for st in r] for r in rows], np.int32).reshape(-1)
    return column(0), column(1), column(2), np.array(n_pairs, np.int32), n_tab


def _attn_kernel(gsz, n_tab,
                 slot_tab, kba_tab, kbb_tab, n_pairs_tab,
                 qz_ref, kt_ref, v_ref, lq1_ref, lk1_ref, lq2_ref, lk2_ref, gain_ref,
                 o_ref, s_a, s_b, m1_ref, m2_ref, acc1_ref, acc2_ref, bias_ref):
    tq = s_a.shape[0] // 2
    tk = s_a.shape[1] // 2
    g = pl.program_id(2)
    s_bufs = (s_a, s_b)
    map_rows = (slice(0, tq), slice(tq, 2 * tq))
    n_pairs = n_pairs_tab[g]

    def entry(tab, t):
        return tab[g * n_tab + t]

    m_refs, acc_refs = (m1_ref, m2_ref), (acc1_ref, acc2_ref)

    def state_rows(slot):
        return pl.ds(pl.multiple_of(slot * tq, tq), tq)

    @pl.when((pl.program_id(0) == 0) & (pl.program_id(1) == 0) & (g == 0))
    def _():
        for m_ref, acc_ref in zip(m_refs, acc_refs):
            m_ref[...] = jnp.full(m_ref.shape, -jnp.inf, jnp.float32)
            acc_ref[...] = jnp.zeros_like(acc_ref)
        row_chunk = lax.broadcasted_iota(jnp.int32, bias_ref.shape, 0) // CHUNK
        col_chunk = lax.broadcasted_iota(jnp.int32, bias_ref.shape, 1) // CHUNK
        bias_ref[...] = jnp.where(col_chunk <= row_chunk, 0.0, -jnp.inf)

    def qk(t, par, n_blocks):
        kt = kt_ref[0, 0, entry(kba_tab, t)]
        if n_blocks == 2:
            kt = jnp.concatenate([kt, kt_ref[0, 0, entry(kbb_tab, t)]], axis=1)
        qrows = state_rows(entry(slot_tab, t))
        for mp, rows in enumerate(map_rows):
            s_bufs[par][rows, 0:n_blocks * tk] = jnp.dot(qz_ref[mp, 0, qrows, :], kt,
                                                         preferred_element_type=jnp.float32)

    def softmax_pv(t, par, kind):
        final = kind != "pair"
        slot = entry(slot_tab, t)
        n_keys = tk if kind == "diag" else 2 * tk
        start = pl.multiple_of(entry(kba_tab, t) * tk, tk)
        vext = jnp.concatenate([v_ref[0, pl.ds(start, n_keys), :],
                                jnp.ones((n_keys, LANES), jnp.bfloat16)], axis=1)
        for mp, rows in enumerate(map_rows):
            s = s_bufs[par][rows, 0:n_keys]
            slabs = [s[:, c * LANES:(c + 1) * LANES] for c in range(n_keys // LANES)]
            if kind != "pair":
                first = (n_keys - tk) // LANES
                slabs[first:] = [sl + bias_ref[:, c * LANES:(c + 1) * LANES]
                                 for c, sl in enumerate(slabs[first:])]
            mx = functools.reduce(jnp.maximum, slabs)
            srows, m_ref, acc_ref = state_rows(slot), m_refs[mp], acc_refs[mp]
            m_old = m_ref[srows, :]
            m_new = jnp.maximum(m_old, jnp.max(mx, axis=1, keepdims=True))
            alpha = jnp.exp2(m_old - m_new)
            p = jnp.concatenate([jnp.exp2(sl - m_new) for sl in slabs], axis=1).astype(jnp.bfloat16)
            acc_ref[srows, :] = (acc_ref[srows, :] * jnp.concatenate([alpha, alpha], axis=1)
                                 + jnp.dot(p, vext, preferred_element_type=jnp.float32))
            m_ref[srows, :] = m_new
        if final:
            lam = (jnp.exp(jnp.sum(lq1_ref[...] * lk1_ref[...], axis=1, keepdims=True))
                   - jnp.exp(jnp.sum(lq2_ref[...] * lk2_ref[...], axis=1, keepdims=True))
                   + LAM_INIT)
            a = []
            for m_ref, acc_ref in zip(m_refs, acc_refs):
                srows = state_rows(slot)
                acc = acc_ref[srows, :]
                a.append(acc[:, :DIFF_V_DIM] / acc[:, DIFF_V_DIM:])
                acc_ref[srows, :] = jnp.zeros((tq, 2 * LANES), jnp.float32)
                m_ref[srows, :] = jnp.full((tq, LANES), -jnp.inf, jnp.float32)
            o = _rms_norm(a[0] - lam * a[1], gain_ref[...]) * (1.0 - LAM_INIT)
            o_ref[0, state_rows(slot), :] = o.astype(o_ref.dtype)

    def run(first_step, n_steps, kind):
        n_blocks = 1 if kind == "diag" else 2

        def trip(c, carry):
            t = first_step + 2 * c
            qk(t + 1, 1, n_blocks)
            softmax_pv(t, 0, kind)
            qk(t + 2, 0, n_blocks)
            softmax_pv(t + 1, 1, kind)
            return carry

        lax.fori_loop(0, n_steps // 2, trip, 0)

    n_half = gsz // 2
    qk(0, 0, 2)
    run(0, n_pairs, "pair")
    run(n_pairs, n_half, "edge")
    run(n_pairs + n_half, n_half, "diag")


def _post_kernel(x1_ref, att_ref, pool_ref, wout_ref, g2_ref, wg_ref, wu_ref, wd_ref, gf_ref,
                 y_ref, acc_ref):
    x2 = (x1_ref[...]
          + jnp.dot(att_ref[...], wout_ref[0:ATT_WIDTH, :], preferred_element_type=jnp.float32)
          + jnp.dot(pool_ref[...], wout_ref[ATT_WIDTH:, :], preferred_element_type=jnp.float32))
    xn = _rms_norm(x2, g2_ref[...]).astype(jnp.bfloat16)
    _swiglu_into(acc_ref, xn, wg_ref, wu_ref, wd_ref)
    x3 = x2 + 0.5 * acc_ref[...]
    y_ref[...] = _rms_norm(x3, gf_ref[...])


def _resident(shape):
    zeros = (0,) * len(shape)
    return pl.BlockSpec(shape, lambda *_: zeros, pipeline_mode=pl.Buffered(1))


def _ffn_weights(w_gate, w_up, w_down):
    wg = w_gate.astype(jnp.bfloat16).reshape(D_MODEL, N_FF_CHUNKS, FF_CHUNK).transpose(1, 0, 2)
    wu = w_up.astype(jnp.bfloat16).reshape(D_MODEL, N_FF_CHUNKS, FF_CHUNK).transpose(1, 0, 2)
    wd = w_down.astype(jnp.bfloat16).reshape(N_FF_CHUNKS, FF_CHUNK, D_MODEL)
    return wg, wu, wd


def _rope_tables(seq):
    half = DIFF_HEAD_DIM // 2
    pos = jnp.arange(seq, dtype=jnp.float32)
    inv_freq = 1.0 / (ROPE_THETA ** (jnp.arange(0, DIFF_HEAD_DIM, 2, dtype=jnp.float32) / DIFF_HEAD_DIM))
    ang = pos[:, None] * inv_freq[None, :]
    cos, sin = jnp.cos(ang), jnp.sin(ang)
    reps = LANES // half
    cos_t = jnp.tile(cos, (1, reps))
    sin_t = jnp.tile(jnp.concatenate([-sin, sin], axis=1), (1, reps // 2))
    return cos_t, sin_t


def kernel(x, ffn1_norm, ffn1_w_gate, ffn1_w_up, ffn1_w_down, mix_norm, w_in, lambda_q1, lambda_k1, lambda_q2, lambda_k2, subln_gain, pool_w, pool_scale, w_out, ffn2_norm, ffn2_w_gate, ffn2_w_up, ffn2_w_down, final_norm):
    B, S, D = x.shape
    assert D == D_MODEL and ffn1_norm.shape[0] == 1
    tm = min(TOKEN_TILE, S)
    tq = tk = tm
    assert S % tm == 0 and tq % CHUNK == 0 and tm >= POOL_HISTORY
    T = B * S
    n_tiles = T // tm
    tiles_per_seq = S // tm
    n_kblocks = n_qblocks = S // tk
    n_heads = N_DIFF_HEADS
    n_groups = ATT_GROUPS
    gsz = n_qblocks // n_groups
    slot_tab, kba_tab, kbb_tab, n_pairs_tab, n_tab = _attn_schedule(n_qblocks, n_groups)
    assert n_qblocks % n_groups == 0 and gsz % 4 == 0
    assert all(n % 2 == 0 for n in n_pairs_tab)
    f32, bf16 = jnp.float32, jnp.bfloat16

    xf = x.reshape(T, D)
    row = lambda a: a.reshape(1, -1).astype(f32)
    wg1, wu1, wd1 = _ffn_weights(ffn1_w_gate[0], ffn1_w_up[0], ffn1_w_down[0])
    wg2, wu2, wd2 = _ffn_weights(ffn2_w_gate[0], ffn2_w_up[0], ffn2_w_down[0])
    cos_t, sin_t = _rope_tables(S)
    params = pltpu.CompilerParams(dimension_semantics=("arbitrary",), vmem_limit_bytes=VMEM_LIMIT_BYTES)

    tok = lambda width: pl.BlockSpec((tm, width), lambda t: (t, 0))
    tab = pl.BlockSpec((tm, LANES), lambda t: (t % tiles_per_seq, 0))
    ffn_specs = [_resident((N_FF_CHUNKS, D, FF_CHUNK)), _resident((N_FF_CHUNKS, D, FF_CHUNK)),
                 _resident((N_FF_CHUNKS, FF_CHUNK, D))]

    kt_spec = pl.BlockSpec((1, n_heads, 1, LANES, tk),
                           lambda t: (t // tiles_per_seq, 0, t % tiles_per_seq, 0, 0))
    x1, qz, kt, v, pool = pl.pallas_call(
        functools.partial(_pre_kernel, tiles_per_seq),
        grid=(n_tiles,),
        in_specs=[tok(D), _resident((1, D))] + ffn_specs
                 + [_resident((1, D)), _resident((D, IN_WIDTH)), tab, tab,
                    _resident((len(POOL_WINDOWS), POOL_GROUP_DIM, POOL_GROUP_DIM)),
                    _resident((1, POOL_WIDTH))],
        out_specs=[tok(D), pl.BlockSpec((2, tm, ATT_WIDTH), lambda t: (0, t, 0)), kt_spec,
                   tok(ATT_WIDTH), tok(POOL_WIDTH)],
        out_shape=[jax.ShapeDtypeStruct((T, D), f32),
                   jax.ShapeDtypeStruct((2, T, ATT_WIDTH), bf16),
                   jax.ShapeDtypeStruct((B, n_heads, n_kblocks, LANES, tk), bf16),
                   jax.ShapeDtypeStruct((T, ATT_WIDTH), bf16),
                   jax.ShapeDtypeStruct((T, POOL_WIDTH), bf16)],
        scratch_shapes=[pltpu.VMEM((tm, D), f32),
                        pltpu.VMEM((tm, 3 * ATT_WIDTH), f32),
                        pltpu.VMEM((POOL_HISTORY + tm, POOL_WIDTH), f32)],
        compiler_params=params,
        name="pre_ffn_proj_pool",
    )(xf, row(ffn1_norm[0]), wg1, wu1, wd1, row(mix_norm[0]), w_in[0].astype(bf16), cos_t, sin_t,
      pool_w[0].astype(bf16), row(pool_scale[0]))

    lam_spec = pl.BlockSpec((1, DIFF_HEAD_DIM), lambda b, h, g, *_: (0, 0))
    att = pl.pallas_call(
        functools.partial(_attn_kernel, gsz, n_tab),
        grid_spec=pltpu.PrefetchScalarGridSpec(
            num_scalar_prefetch=4,
            grid=(B, n_heads, n_groups),
            in_specs=[pl.BlockSpec((2, 1, gsz * tq, LANES), lambda b, h, g, *_: (0, b, g, h)),
                      pl.BlockSpec((1, 1, n_kblocks, LANES, tk), lambda b, h, g, *_: (b, h, 0, 0, 0)),
                      pl.BlockSpec((1, S, LANES), lambda b, h, g, *_: (b, 0, h)),
                      lam_spec, lam_spec, lam_spec, lam_spec,
                      pl.BlockSpec((1, DIFF_V_DIM), lambda b, h, g, *_: (0, 0))],
            out_specs=pl.BlockSpec((1, gsz * tq, LANES), lambda b, h, g, *_: (b, g, h)),
            scratch_shapes=[pltpu.VMEM((2 * tq, 2 * tk), f32), pltpu.VMEM((2 * tq, 2 * tk), f32),
                            pltpu.VMEM((gsz * tq, LANES), f32), pltpu.VMEM((gsz * tq, LANES), f32),
                            pltpu.VMEM((gsz * tq, 2 * LANES), f32),
                            pltpu.VMEM((gsz * tq, 2 * LANES), f32),
                            pltpu.VMEM((tq, tk), f32)]),
        out_shape=jax.ShapeDtypeStruct((B, S, ATT_WIDTH), bf16),
        compiler_params=pltpu.CompilerParams(
            dimension_semantics=("arbitrary", "arbitrary", "arbitrary"),
            vmem_limit_bytes=VMEM_LIMIT_BYTES),
        name="diff_flash_attention",
    )(jnp.asarray(slot_tab), jnp.asarray(kba_tab), jnp.asarray(kbb_tab), jnp.asarray(n_pairs_tab),
      qz.reshape(2, B, S, ATT_WIDTH), kt, v.reshape(B, S, ATT_WIDTH),
      row(lambda_q1[0]), row(lambda_k1[0]), row(lambda_q2[0]), row(lambda_k2[0]), row(subln_gain[0]))

    y = pl.pallas_call(
        _post_kernel,
        grid=(n_tiles,),
        in_specs=[tok(D), tok(ATT_WIDTH), tok(POOL_WIDTH), _resident((D, D)), _resident((1, D))]
                 + ffn_specs + [_resident((1, D))],
        out_specs=tok(D),
        out_shape=jax.ShapeDtypeStruct((T, D), f32),
        scratch_shapes=[pltpu.VMEM((tm, D), f32)],
        compiler_params=params,
        name="post_proj_ffn_norm",
    )(x1, att.reshape(T, ATT_WIDTH), pool, w_out[0].astype(bf16), row(ffn2_norm[0]),
      wg2, wu2, wd2, row(final_norm))
    return y.reshape(B, S, D)
```

```python
import functools
import math

import jax
import jax.numpy as jnp
import numpy as np
from jax import lax
from jax.experimental import pallas as pl
from jax.experimental.pallas import tpu as pltpu

D_MODEL = 1024
CHUNK = 64
ATT_WIDTH = 512
POOL_WIDTH = 512
N_DIFF_HEADS = 4
DIFF_HEAD_DIM = 64
DIFF_V_DIM = 128
POOL_WINDOWS = (2, 4, 8, 16)
POOL_GROUP_DIM = 128
D_FF = 2816
ROPE_THETA = 10000.0
NORM_EPS = 1e-6
IN_WIDTH = 2048
LAM_INIT = 0.8 - 0.6 * math.exp(-0.3 * 0)
Q_SCALE = DIFF_HEAD_DIM ** -0.5 * math.log2(math.e)

LANES = 128
SUBLANES = 8
VMEM_LIMIT_BYTES = 56 * 1024 * 1024

FF_CHUNK = 256
N_FF_CHUNKS = D_FF // FF_CHUNK
TOKEN_TILE = 512
POST_TILES_PER_STEP = 2
POOL_HISTORY = 16
ATT_BLOCK = TOKEN_TILE
ATT_GROUPS = 2


def _rms_norm(x, g):
    return x * lax.rsqrt(jnp.mean(x * x, axis=-1, keepdims=True) + NORM_EPS) * g


def _swiglu_into(acc_ref, xn, wg_ref, wu_ref, wd_ref):
    acc_ref[...] = jnp.zeros_like(acc_ref)
    for c in range(N_FF_CHUNKS):
        cols = slice(c * FF_CHUNK, (c + 1) * FF_CHUNK)
        g = jnp.dot(xn, wg_ref[:, cols], preferred_element_type=jnp.float32)
        u = jnp.dot(xn, wu_ref[:, cols], preferred_element_type=jnp.float32)
        a = (g * jax.nn.sigmoid(g) * u).astype(jnp.bfloat16)
        acc_ref[...] += jnp.dot(a, wd_ref[cols, :], preferred_element_type=jnp.float32)


def _rope_slab(t, cos, sin_signed):
    lane = lax.broadcasted_iota(jnp.int32, t.shape, 1)
    first_half = (lane % DIFF_HEAD_DIM) < (DIFF_HEAD_DIM // 2)
    partner = jnp.where(first_half,
                        pltpu.roll(t, LANES - DIFF_HEAD_DIM // 2, 1),
                        pltpu.roll(t, DIFF_HEAD_DIM // 2, 1))
    return t * cos + partner * sin_signed


def _pre_kernel(tiles_per_seq,
                x_ref, g1_ref, wg_ref, wu_ref, wd_ref, gmix_ref, win_ref, cos_ref, sin_ref,
                poolw_ref, pscale_ref,
                x1_ref, qz_ref, kt_ref, v_ref, pool_ref,
                acc_ref, qkv_ref, ext_ref):
    tm = x_ref.shape[0]
    x = x_ref[...]
    xn = _rms_norm(x, g1_ref[...]).astype(jnp.bfloat16)
    _swiglu_into(acc_ref, xn, wg_ref, wu_ref, wd_ref)
    x1 = x + 0.5 * acc_ref[...]
    x1_ref[...] = x1

    h = _rms_norm(x1, gmix_ref[...]).astype(jnp.bfloat16)
    cos = cos_ref[...]
    sin = sin_ref[...]
    qkv_ref[...] = jnp.dot(h, win_ref[:, :3 * ATT_WIDTH], preferred_element_type=jnp.float32)
    lane = lax.broadcasted_iota(jnp.int32, (tm, LANES), 1)
    zero = jnp.zeros((tm, LANES), jnp.bfloat16)
    for hd in range(N_DIFF_HEADS):
        cols = slice(hd * LANES, (hd + 1) * LANES)
        q = (_rope_slab(qkv_ref[:, cols], cos, sin) * Q_SCALE).astype(jnp.bfloat16)
        qz_ref[0, :, cols] = jnp.where(lane < DIFF_HEAD_DIM, q, zero)
        qz_ref[1, :, cols] = jnp.where(lane >= DIFF_HEAD_DIM, q, zero)
        kcols = slice(ATT_WIDTH + hd * LANES, ATT_WIDTH + (hd + 1) * LANES)
        kt_ref[0, hd, 0] = _rope_slab(qkv_ref[:, kcols], cos, sin).T.astype(jnp.bfloat16)
    v_ref[...] = qkv_ref[:, 2 * ATT_WIDTH:].astype(jnp.bfloat16)

    tile_in_seq = pl.program_id(0) % tiles_per_seq

    @pl.when(tile_in_seq == 0)
    def _():
        ext_ref[0:POOL_HISTORY, :] = jnp.zeros((POOL_HISTORY, POOL_WIDTH), jnp.float32)

    ext_ref[POOL_HISTORY:, :] = jnp.dot(h, win_ref[:, 3 * ATT_WIDTH:],
                                        preferred_element_type=jnp.float32)
    pos = tile_in_seq * tm + lax.broadcasted_iota(jnp.int32, (tm, 1), 0)
    for gi, win in enumerate(POOL_WINDOWS):
        cols = slice(gi * POOL_GROUP_DIM, (gi + 1) * POOL_GROUP_DIM)
        ext = ext_ref[:, cols]
        wsum = ext
        shift = 1
        while shift < win:
            wsum = wsum + pltpu.roll(wsum, shift, 0)
            shift *= 2
        inv_cnt = 1.0 / jnp.minimum(pos + 1, win).astype(jnp.float32)
        d = (wsum[POOL_HISTORY:] * inv_cnt - ext[POOL_HISTORY:]).astype(jnp.bfloat16)
        y = jnp.dot(d, poolw_ref[gi], preferred_element_type=jnp.float32)
        pool_ref[:, cols] = (y * pscale_ref[:, cols]).astype(jnp.bfloat16)
    ext_ref[0:POOL_HISTORY, :] = ext_ref[tm:tm + POOL_HISTORY, :]


def _attn_schedule(n_qblocks, n_groups):
    gsz = n_qblocks // n_groups
    rows, n_pairs = [], []
    for g in range(n_groups):
        qblocks = range(g * gsz, (g + 1) * gsz)
        steps = [(i - g * gsz, 2 * c, 2 * c + 1) for i in qblocks for c in range(i // 2)]
        n_pairs.append(len(steps))
        steps += [(i - g * gsz, i - 1, i) for i in qblocks if i % 2 == 1]
        steps += [(i - g * gsz, i, min(i + 1, n_qblocks - 1)) for i in qblocks if i % 2 == 0]
        steps.append(steps[-1])
        rows.append(steps)
    n_tab = max(len(r) for r in rows)
    rows = [r + [r[-1]] * (n_tab - len(r)) for r in rows]
    column = lambda k: np.array([[st[k] for st in r] for r in rows], np.int32).reshape(-1)
    return column(0), column(1), column(2), np.array(n_pairs, np.int32), n_tab


def _attn_kernel(gsz, n_tab, pair_steps_per_trip,
                 slot_tab, kba_tab, kbb_tab, n_pairs_tab,
                 qz_ref, kt_ref, v_ref, lq1_ref, lk1_ref, lq2_ref, lk2_ref, gain_ref,
                 o_ref, s_a, s_b, mx_a, mx_b, m1_ref, m2_ref, acc1_ref, acc2_ref, bias_ref):
    tq = s_a.shape[0] // 2
    tk = s_a.shape[1] // 2
    g = pl.program_id(2)
    s_bufs, mx_bufs = (s_a, s_b), (mx_a, mx_b)
    map_rows = (slice(0, tq), slice(tq, 2 * tq))
    n_pairs = n_pairs_tab[g]

    def entry(tab, t):
        return tab[g * n_tab + t]

    m_refs, acc_refs = (m1_ref, m2_ref), (acc1_ref, acc2_ref)

    def state_rows(slot):
        return pl.ds(pl.multiple_of(slot * tq, tq), tq)

    @pl.when((pl.program_id(0) == 0) & (pl.program_id(1) == 0) & (g == 0))
    def _():
        for m_ref, acc_ref in zip(m_refs, acc_refs):
            m_ref[...] = jnp.full(m_ref.shape, -jnp.inf, jnp.float32)
            acc_ref[...] = jnp.zeros_like(acc_ref)
        row_chunk = lax.broadcasted_iota(jnp.int32, bias_ref.shape, 0) // CHUNK
        col_chunk = lax.broadcasted_iota(jnp.int32, bias_ref.shape, 1) // CHUNK
        bias_ref[...] = jnp.where(col_chunk <= row_chunk, 0.0, -jnp.inf)

    def qk(t, par, n_blocks, with_max):
        kt = kt_ref[0, 0, entry(kba_tab, t)]
        if n_blocks == 2:
            kt = jnp.concatenate([kt, kt_ref[0, 0, entry(kbb_tab, t)]], axis=1)
        qrows = state_rows(entry(slot_tab, t))
        for mp, rows in enumerate(map_rows):
            s = jnp.dot(qz_ref[mp, 0, qrows, :], kt, preferred_element_type=jnp.float32)
            s_bufs[par][rows, 0:n_blocks * tk] = s
            if with_max:
                mx_bufs[par][rows, :] = functools.reduce(
                    jnp.maximum, [s[:, c * LANES:(c + 1) * LANES] for c in range(s.shape[1] // LANES)])

    def softmax_pv(t, par, kind):
        final = kind != "pair"
        slot = entry(slot_tab, t)
        n_keys = tk if kind == "diag" else 2 * tk
        start = pl.multiple_of(entry(kba_tab, t) * tk, tk)
        vext = jnp.concatenate([v_ref[0, pl.ds(start, n_keys), :],
                                jnp.ones((n_keys, LANES), jnp.bfloat16)], axis=1)
        for mp, rows in enumerate(map_rows):
            s = s_bufs[par][rows, 0:n_keys]
            slabs = [s[:, c * LANES:(c + 1) * LANES] for c in range(n_keys // LANES)]
            if kind == "pair":
                mx = mx_bufs[par][rows, :]
            else:
                first = (n_keys - tk) // LANES
                slabs[first:] = [sl + bias_ref[:, c * LANES:(c + 1) * LANES]
                                 for c, sl in enumerate(slabs[first:])]
                mx = functools.reduce(jnp.maximum, slabs)
            srows, m_ref, acc_ref = state_rows(slot), m_refs[mp], acc_refs[mp]
            m_old = m_ref[srows, :]
            m_new = jnp.maximum(m_old, jnp.max(mx, axis=1, keepdims=True))
            alpha = jnp.exp2(m_old - m_new)
            p = jnp.concatenate([jnp.exp2(sl - m_new) for sl in slabs], axis=1).astype(jnp.bfloat16)
            acc_ref[srows, :] = (acc_ref[srows, :] * jnp.concatenate([alpha, alpha], axis=1)
                                 + jnp.dot(p, vext, preferred_element_type=jnp.float32))
            m_ref[srows, :] = m_new
        if final:
            lam = (jnp.exp(jnp.sum(lq1_ref[...] * lk1_ref[...], axis=1, keepdims=True))
                   - jnp.exp(jnp.sum(lq2_ref[...] * lk2_ref[...], axis=1, keepdims=True))
                   + LAM_INIT)
            a = []
            for m_ref, acc_ref in zip(m_refs, acc_refs):
                srows = state_rows(slot)
                acc = acc_ref[srows, :]
                a.append(acc[:, :DIFF_V_DIM] / acc[:, DIFF_V_DIM:])
                acc_ref[srows, :] = jnp.zeros((tq, 2 * LANES), jnp.float32)
                m_ref[srows, :] = jnp.full((tq, LANES), -jnp.inf, jnp.float32)
            o = _rms_norm(a[0] - lam * a[1], gain_ref[...]) * (1.0 - LAM_INIT)
            o_ref[0, state_rows(slot), :] = o.astype(o_ref.dtype)

    def run(first_step, n_steps, kind, steps_per_trip=2):
        n_blocks = 1 if kind == "diag" else 2
        with_max = kind == "pair"

        def trip(c, carry):
            t = first_step + steps_per_trip * c
            for k in range(steps_per_trip):
                qk(t + k + 1, (k + 1) % 2, n_blocks, with_max)
                softmax_pv(t + k, k % 2, kind)
            return carry

        lax.fori_loop(0, n_steps // steps_per_trip, trip, 0)

    n_half = gsz // 2
    qk(0, 0, 2, True)
    run(0, n_pairs, "pair", pair_steps_per_trip)
    run(n_pairs, n_half, "edge")
    run(n_pairs + n_half, n_half, "diag")


def _post_kernel(tm, x1_ref, att_ref, pool_ref, wout_ref, g2_ref, wg_ref, wu_ref, wd_ref, gf_ref,
                 y_ref, acc_ref):
    for r0 in range(0, x1_ref.shape[0], tm):
        rows = pl.ds(r0, tm)
        acc = acc_ref.at[rows]
        x2 = (x1_ref[rows, :]
              + jnp.dot(att_ref[rows, :], wout_ref[0:ATT_WIDTH, :], preferred_element_type=jnp.float32)
              + jnp.dot(pool_ref[rows, :], wout_ref[ATT_WIDTH:, :], preferred_element_type=jnp.float32))
        xn = _rms_norm(x2, g2_ref[...]).astype(jnp.bfloat16)
        _swiglu_into(acc, xn, wg_ref, wu_ref, wd_ref)
        x3 = x2 + 0.5 * acc[...]
        y_ref[rows, :] = _rms_norm(x3, gf_ref[...])


def _resident(shape):
    zeros = (0,) * len(shape)
    return pl.BlockSpec(shape, lambda *_: zeros, pipeline_mode=pl.Buffered(1))


def _ffn_weights(w_gate, w_up, w_down):
    return w_gate.astype(jnp.bfloat16), w_up.astype(jnp.bfloat16), w_down.astype(jnp.bfloat16)


def _rope_tables(seq):
    half = DIFF_HEAD_DIM // 2
    pos = jnp.arange(seq, dtype=jnp.float32)
    inv_freq = 1.0 / (ROPE_THETA ** (jnp.arange(0, DIFF_HEAD_DIM, 2, dtype=jnp.float32) / DIFF_HEAD_DIM))
    ang = pos[:, None] * inv_freq[None, :]
    cos, sin = jnp.cos(ang), jnp.sin(ang)
    reps = LANES // half
    cos_t = jnp.tile(cos, (1, reps))
    sin_t = jnp.tile(jnp.concatenate([-sin, sin], axis=1), (1, reps // 2))
    return cos_t, sin_t


def kernel(x, ffn1_norm, ffn1_w_gate, ffn1_w_up, ffn1_w_down, mix_norm, w_in, lambda_q1, lambda_k1, lambda_q2, lambda_k2, subln_gain, pool_w, pool_scale, w_out, ffn2_norm, ffn2_w_gate, ffn2_w_up, ffn2_w_down, final_norm):
    B, S, D = x.shape
    assert D == D_MODEL and ffn1_norm.shape[0] == 1
    tm = min(TOKEN_TILE, S)
    tq = tk = tm
    assert S % tm == 0 and tq % CHUNK == 0 and tm >= POOL_HISTORY
    T = B * S
    n_tiles = T // tm
    tiles_per_seq = S // tm
    n_kblocks = n_qblocks = S // tk
    n_heads = N_DIFF_HEADS
    n_groups = ATT_GROUPS
    gsz = n_qblocks // n_groups
    slot_tab, kba_tab, kbb_tab, n_pairs_tab, n_tab = _attn_schedule(n_qblocks, n_groups)
    assert n_qblocks % n_groups == 0 and gsz % 4 == 0
    assert all(n % 2 == 0 for n in n_pairs_tab)
    pair_steps_per_trip = 2
    f32, bf16 = jnp.float32, jnp.bfloat16

    xf = x.reshape(T, D)
    row = lambda a: a.reshape(1, -1).astype(f32)
    wg1, wu1, wd1 = _ffn_weights(ffn1_w_gate[0], ffn1_w_up[0], ffn1_w_down[0])
    wg2, wu2, wd2 = _ffn_weights(ffn2_w_gate[0], ffn2_w_up[0], ffn2_w_down[0])
    cos_t, sin_t = _rope_tables(S)
    params = pltpu.CompilerParams(dimension_semantics=("arbitrary",), vmem_limit_bytes=VMEM_LIMIT_BYTES)

    tok = lambda width: pl.BlockSpec((tm, width), lambda t: (t, 0))
    tab = pl.BlockSpec((tm, LANES), lambda t: (t % tiles_per_seq, 0))
    ffn_specs = [_resident((D, D_FF)), _resident((D, D_FF)), _resident((D_FF, D))]

    kt_spec = pl.BlockSpec((1, n_heads, 1, LANES, tk),
                           lambda t: (t // tiles_per_seq, 0, t % tiles_per_seq, 0, 0))
    x1, qz, kt, v, pool = pl.pallas_call(
        functools.partial(_pre_kernel, tiles_per_seq),
        grid=(n_tiles,),
        in_specs=[tok(D), _resident((1, D))] + ffn_specs
                 + [_resident((1, D)), _resident((D, IN_WIDTH)), tab, tab,
                    _resident((len(POOL_WINDOWS), POOL_GROUP_DIM, POOL_GROUP_DIM)),
                    _resident((1, POOL_WIDTH))],
        out_specs=[tok(D), pl.BlockSpec((2, tm, ATT_WIDTH), lambda t: (0, t, 0)), kt_spec,
                   tok(ATT_WIDTH), tok(POOL_WIDTH)],
        out_shape=[jax.ShapeDtypeStruct((T, D), f32),
                   jax.ShapeDtypeStruct((2, T, ATT_WIDTH), bf16),
                   jax.ShapeDtypeStruct((B, n_heads, n_kblocks, LANES, tk), bf16),
                   jax.ShapeDtypeStruct((T, ATT_WIDTH), bf16),
                   jax.ShapeDtypeStruct((T, POOL_WIDTH), bf16)],
        scratch_shapes=[pltpu.VMEM((tm, D), f32),
                        pltpu.VMEM((tm, 3 * ATT_WIDTH), f32),
                        pltpu.VMEM((POOL_HISTORY + tm, POOL_WIDTH), f32)],
        compiler_params=params,
        name="pre_ffn_proj_pool",
    )(xf, row(ffn1_norm[0]), wg1, wu1, wd1, row(mix_norm[0]), w_in[0].astype(bf16), cos_t, sin_t,
      pool_w[0].astype(bf16), row(pool_scale[0]))

    lam_spec = pl.BlockSpec((1, DIFF_HEAD_DIM), lambda b, h, g, *_: (0, 0))
    att = pl.pallas_call(
        functools.partial(_attn_kernel, gsz, n_tab, pair_steps_per_trip),
        grid_spec=pltpu.PrefetchScalarGridSpec(
            num_scalar_prefetch=4,
            grid=(B, n_heads, n_groups),
            in_specs=[pl.BlockSpec((2, 1, gsz * tq, LANES), lambda b, h, g, *_: (0, b, g, h)),
                      pl.BlockSpec((1, 1, n_kblocks, LANES, tk), lambda b, h, g, *_: (b, h, 0, 0, 0)),
                      pl.BlockSpec((1, S, LANES), lambda b, h, g, *_: (b, 0, h)),
                      lam_spec, lam_spec, lam_spec, lam_spec,
                      pl.BlockSpec((1, DIFF_V_DIM), lambda b, h, g, *_: (0, 0))],
            out_specs=pl.BlockSpec((1, gsz * tq, LANES), lambda b, h, g, *_: (b, g, h)),
            scratch_shapes=[pltpu.VMEM((2 * tq, 2 * tk), f32), pltpu.VMEM((2 * tq, 2 * tk), f32),
                            pltpu.VMEM((2 * tq, LANES), f32), pltpu.VMEM((2 * tq, LANES), f32),
                            pltpu.VMEM((gsz * tq, LANES), f32), pltpu.VMEM((gsz * tq, LANES), f32),
                            pltpu.VMEM((gsz * tq, 2 * LANES), f32),
                            pltpu.VMEM((gsz * tq, 2 * LANES), f32),
                            pltpu.VMEM((tq, tk), f32)]),
        out_shape=jax.ShapeDtypeStruct((B, S, ATT_WIDTH), bf16),
        compiler_params=pltpu.CompilerParams(
            dimension_semantics=("arbitrary", "arbitrary", "arbitrary"),
            vmem_limit_bytes=VMEM_LIMIT_BYTES),
        name="diff_flash_attention",
    )(jnp.asarray(slot_tab), jnp.asarray(kba_tab), jnp.asarray(kbb_tab), jnp.asarray(n_pairs_tab),
      qz.reshape(2, B, S, ATT_WIDTH), kt, v.reshape(B, S, ATT_WIDTH),
      row(lambda_q1[0]), row(lambda_k1[0]), row(lambda_q2[0]), row(lambda_k2[0]), row(subln_gain[0]))

    post_rows = POST_TILES_PER_STEP * tm
    assert T % post_rows == 0
    tok2 = lambda width: pl.BlockSpec((post_rows, width), lambda t: (t, 0))
    y = pl.pallas_call(
        functools.partial(_post_kernel, tm),
        grid=(T // post_rows,),
        in_specs=[tok2(D), tok2(ATT_WIDTH), tok2(POOL_WIDTH), _resident((D, D)), _resident((1, D))]
                 + ffn_specs + [_resident((1, D))],
        out_specs=tok2(D),
        out_shape=jax.ShapeDtypeStruct((T, D), f32),
        scratch_shapes=[pltpu.VMEM((post_rows, D), f32)],
        compiler_params=params,
        name="post_proj_ffn_norm",
    )(x1, att.reshape(T, ATT_WIDTH), pool, w_out[0].astype(bf16), row(ffn2_norm[0]),
      wg2, wu2, wd2, row(final_norm))
    return y.reshape(B, S, D)
```

```python
import functools
import math

import jax
import jax.numpy as jnp
import numpy as np
from jax import lax
from jax.experimental import pallas as pl
from jax.experimental.pallas import tpu as pltpu

D_MODEL = 1024
CHUNK = 64
ATT_WIDTH = 512
POOL_WIDTH = 512
N_DIFF_HEADS = 4
DIFF_HEAD_DIM = 64
DIFF_V_DIM = 128
POOL_WINDOWS = (2, 4, 8, 16)
POOL_GROUP_DIM = 128
D_FF = 2816
ROPE_THETA = 10000.0
NORM_EPS = 1e-6
IN_WIDTH = 2048
LAM_INIT = 0.8 - 0.6 * math.exp(-0.3 * 0)
Q_SCALE = DIFF_HEAD_DIM ** -0.5 * math.log2(math.e)

LANES = 128
SUBLANES = 8
VMEM_LIMIT_BYTES = 56 * 1024 * 1024

FF_CHUNK = 256
N_FF_CHUNKS = D_FF // FF_CHUNK
TOKEN_TILE = 512
POST_TILES_PER_STEP = 2
POOL_HISTORY = 16
ATT_BLOCK = TOKEN_TILE
ATT_GROUPS = 2


def _rms_norm(x, g):
    return x * lax.rsqrt(jnp.mean(x * x, axis=-1, keepdims=True) + NORM_EPS) * g


def _swiglu_into(acc_ref, xn, wg_ref, wu_ref, wd_ref, first=0, last=N_FF_CHUNKS):
    if first == 0:
        acc_ref[...] = jnp.zeros_like(acc_ref)
    for c in range(first, last):
        cols = slice(c * FF_CHUNK, (c + 1) * FF_CHUNK)
        g = jnp.dot(xn, wg_ref[:, cols], preferred_element_type=jnp.float32)
        u = jnp.dot(xn, wu_ref[:, cols], preferred_element_type=jnp.float32)
        a = (g * jax.nn.sigmoid(g) * u).astype(jnp.bfloat16)
        acc_ref[...] += jnp.dot(a, wd_ref[cols, :], preferred_element_type=jnp.float32)


def _rope_slab(t, cos, sin_signed):
    lane = lax.broadcasted_iota(jnp.int32, t.shape, 1)
    first_half = (lane % DIFF_HEAD_DIM) < (DIFF_HEAD_DIM // 2)
    partner = jnp.where(first_half,
                        pltpu.roll(t, LANES - DIFF_HEAD_DIM // 2, 1),
                        pltpu.roll(t, DIFF_HEAD_DIM // 2, 1))
    return t * cos + partner * sin_signed


def _pre_kernel(tiles_per_seq,
                x_ref, g1_ref, wg_ref, wu_ref, wd_ref, gmix_ref, win_ref, cos_ref, sin_ref,
                poolw_ref, pscale_ref,
                x1_ref, qz_ref, kt_ref, v_ref, pool_ref,
                acc_ref, ext_ref):
    tm = x_ref.shape[0]
    tile_in_seq = pl.program_id(0) % tiles_per_seq

    @pl.when(tile_in_seq == 0)
    def _():
        ext_ref[0:POOL_HISTORY, :] = jnp.zeros((POOL_HISTORY, POOL_WIDTH), jnp.float32)

    x = x_ref[...]
    xn = _rms_norm(x, g1_ref[...]).astype(jnp.bfloat16)
    _swiglu_into(acc_ref, xn, wg_ref, wu_ref, wd_ref)
    x1 = x + 0.5 * acc_ref[...]
    x1_ref[...] = x1

    h = _rms_norm(x1, gmix_ref[...]).astype(jnp.bfloat16)
    cos = cos_ref[...]
    sin = sin_ref[...]
    lane = lax.broadcasted_iota(jnp.int32, (tm, LANES), 1)
    zero = jnp.zeros((tm, LANES), jnp.bfloat16)

    ext_ref[POOL_HISTORY:, :] = jnp.dot(h, win_ref[:, 3 * ATT_WIDTH:],
                                        preferred_element_type=jnp.float32)
    pos = tile_in_seq * tm + lax.broadcasted_iota(jnp.int32, (tm, 1), 0)
    for gi, win in enumerate(POOL_WINDOWS):
        cols = slice(gi * POOL_GROUP_DIM, (gi + 1) * POOL_GROUP_DIM)
        ext = ext_ref[:, cols]
        wsum = ext
        shift = 1
        while shift < win:
            wsum = wsum + pltpu.roll(wsum, shift, 0)
            shift *= 2
        inv_cnt = 1.0 / jnp.minimum(pos + 1, win).astype(jnp.float32)
        d = (wsum[POOL_HISTORY:] * inv_cnt - ext[POOL_HISTORY:]).astype(jnp.bfloat16)
        y = jnp.dot(d, poolw_ref[gi], preferred_element_type=jnp.float32)
        pool_ref[:, cols] = (y * pscale_ref[:, cols]).astype(jnp.bfloat16)

        if gi < 3:
            pcols = slice(gi * ATT_WIDTH, (gi + 1) * ATT_WIDTH)
            proj = jnp.dot(h, win_ref[:, pcols], preferred_element_type=jnp.float32)
        if gi == 0:
            for hd in range(N_DIFF_HEADS):
                hcols = slice(hd * LANES, (hd + 1) * LANES)
                q = (_rope_slab(proj[:, hcols], cos, sin) * Q_SCALE).astype(jnp.bfloat16)
                qz_ref[0, :, hcols] = jnp.where(lane < DIFF_HEAD_DIM, q, zero)
                qz_ref[1, :, hcols] = jnp.where(lane >= DIFF_HEAD_DIM, q, zero)
        elif gi == 1:
            for hd in range(N_DIFF_HEADS):
                hcols = slice(hd * LANES, (hd + 1) * LANES)
                kt_ref[0, hd, 0] = _rope_slab(proj[:, hcols], cos, sin).T.astype(jnp.bfloat16)
        elif gi == 2:
            v_ref[...] = proj.astype(jnp.bfloat16)
    ext_ref[0:POOL_HISTORY, :] = ext_ref[tm:tm + POOL_HISTORY, :]


def _attn_schedule(n_qblocks, n_groups):
    gsz = n_qblocks // n_groups
    rows, n_pairs = [], []
    for g in range(n_groups):
        qblocks = range(g * gsz, (g + 1) * gsz)
        steps = [(i - g * gsz, 2 * c, 2 * c + 1) for i in qblocks for c in range(i // 2)]
        n_pairs.append(len(steps))
        steps += [(i - g * gsz, i - 1, i) for i in qblocks if i % 2 == 1]
        steps += [(i - g * gsz, i, min(i + 1, n_qblocks - 1)) for i in qblocks if i % 2 == 0]
        steps.append(steps[-1])
        rows.append(steps)
    n_tab = max(len(r) for r in rows)
    rows = [r + [r[-1]] * (n_tab - len(r)) for r in rows]
    column = lambda k: np.array([[st[k] for st in r] for r in rows], np.int32).reshape(-1)
    return column(0), column(1), column(2), np.array(n_pairs, np.int32), n_tab


def _attn_kernel(gsz, n_tab,
                 slot_tab, kba_tab, kbb_tab, n_pairs_tab,
                 qz_ref, kt_ref, v_ref, lq1_ref, lk1_ref, lq2_ref, lk2_ref, gain_ref,
                 o_ref, s_a, s_b, mx_a, mx_b, m1_ref, m2_ref, acc1_ref, acc2_ref, bias_ref):
    tq = s_a.shape[0] // 2
    tk = s_a.shape[1] // 2
    g = pl.program_id(2)
    s_bufs, mx_bufs = (s_a, s_b), (mx_a, mx_b)
    map_rows = (slice(0, tq), slice(tq, 2 * tq))
    n_pairs = n_pairs_tab[g]

    def entry(tab, t):
        return tab[g * n_tab + t]

    m_refs, acc_refs = (m1_ref, m2_ref), (acc1_ref, acc2_ref)

    def state_rows(slot):
        return pl.ds(pl.multiple_of(slot * tq, tq), tq)

    @pl.when((pl.program_id(0) == 0) & (pl.program_id(1) == 0) & (g == 0))
    def _():
        for m_ref, acc_ref in zip(m_refs, acc_refs):
            m_ref[...] = jnp.full(m_ref.shape, -jnp.inf, jnp.float32)
            acc_ref[...] = jnp.zeros_like(acc_ref)
        row_chunk = lax.broadcasted_iota(jnp.int32, bias_ref.shape, 0) // CHUNK
        col_chunk = lax.broadcasted_iota(jnp.int32, bias_ref.shape, 1) // CHUNK
        bias_ref[...] = jnp.where(col_chunk <= row_chunk, 0.0, -jnp.inf)

    def qk(t, par, n_blocks, with_max, maps=(0, 1)):
        kt = kt_ref[0, 0, entry(kba_tab, t)]
        if n_blocks == 2:
            kt = jnp.concatenate([kt, kt_ref[0, 0, entry(kbb_tab, t)]], axis=1)
        qrows = state_rows(entry(slot_tab, t))
        for mp in maps:
            rows = map_rows[mp]
            s = jnp.dot(qz_ref[mp, 0, qrows, :], kt, preferred_element_type=jnp.float32)
            s_bufs[par][rows, 0:n_blocks * tk] = s
            if with_max:
                mx_bufs[par][rows, :] = functools.reduce(
                    jnp.maximum, [s[:, c * LANES:(c + 1) * LANES] for c in range(s.shape[1] // LANES)])

    def n_keys_of(kind):
        return tk if kind == "diag" else 2 * tk

    def softmax(t, par, kind, mp):
        rows, n_keys = map_rows[mp], n_keys_of(kind)
        s = s_bufs[par][rows, 0:n_keys]
        slabs = [s[:, c * LANES:(c + 1) * LANES] for c in range(n_keys // LANES)]
        if kind == "pair":
            mx = mx_bufs[par][rows, :]
        else:
            first = (n_keys - tk) // LANES
            slabs[first:] = [sl + bias_ref[:, c * LANES:(c + 1) * LANES]
                             for c, sl in enumerate(slabs[first:])]
            mx = functools.reduce(jnp.maximum, slabs)
        m_old = m_refs[mp][state_rows(entry(slot_tab, t)), :]
        m_new = jnp.maximum(m_old, jnp.max(mx, axis=1, keepdims=True))
        alpha = jnp.exp2(m_old - m_new)
        p = jnp.concatenate([jnp.exp2(sl - m_new) for sl in slabs], axis=1).astype(jnp.bfloat16)
        return p, alpha, m_new

    def pv_acc(t, kind, mp, stats):
        p, alpha, m_new = stats
        n_keys = n_keys_of(kind)
        start = pl.multiple_of(entry(kba_tab, t) * tk, tk)
        vext = jnp.concatenate([v_ref[0, pl.ds(start, n_keys), :],
                                jnp.ones((n_keys, LANES), jnp.bfloat16)], axis=1)
        srows, m_ref, acc_ref = state_rows(entry(slot_tab, t)), m_refs[mp], acc_refs[mp]
        acc_ref[srows, :] = (acc_ref[srows, :] * jnp.concatenate([alpha, alpha], axis=1)
                             + jnp.dot(p, vext, preferred_element_type=jnp.float32))
        m_ref[srows, :] = m_new

    def finalize(t):
        srows = state_rows(entry(slot_tab, t))
        lam = (jnp.exp(jnp.sum(lq1_ref[...] * lk1_ref[...], axis=1, keepdims=True))
               - jnp.exp(jnp.sum(lq2_ref[...] * lk2_ref[...], axis=1, keepdims=True))
               + LAM_INIT)
        a = []
        for m_ref, acc_ref in zip(m_refs, acc_refs):
            acc = acc_ref[srows, :]
            a.append(acc[:, :DIFF_V_DIM] / acc[:, DIFF_V_DIM:])
            acc_ref[srows, :] = jnp.zeros((tq, 2 * LANES), jnp.float32)
            m_ref[srows, :] = jnp.full((tq, LANES), -jnp.inf, jnp.float32)
        o = _rms_norm(a[0] - lam * a[1], gain_ref[...]) * (1.0 - LAM_INIT)
        o_ref[0, srows, :] = o.astype(o_ref.dtype)

    def run(first_step, n_steps, kind, steps_per_trip=2):
        n_blocks = 1 if kind == "diag" else 2
        with_max = kind == "pair"

        def trip(c, carry):
            t = first_step + steps_per_trip * c
            for k in range(steps_per_trip):
                cur, nxt = k % 2, (k + 1) % 2
                for mp in (0, 1):
                    qk(t + k + 1, nxt, n_blocks, with_max, (mp,))
                    pv_acc(t + k, kind, mp, softmax(t + k, cur, kind, mp))
                if kind != "pair":
                    finalize(t + k)
            return carry

        lax.fori_loop(0, n_steps // steps_per_trip, trip, 0)

    n_half = gsz // 2
    qk(0, 0, 2, True)
    run(0, n_pairs, "pair")
    run(n_pairs, n_half, "edge")
    run(n_pairs + n_half, n_half, "diag")


def _post_kernel(tm, x1_ref, att_ref, pool_ref, wout_ref, g2_ref, wg_ref, wu_ref, wd_ref, gf_ref,
                 y_ref, acc_ref):
    tiles = [pl.ds(r0, tm) for r0 in range(0, x1_ref.shape[0], tm)]
    xn = {}

    def head(i):
        rows = tiles[i]
        x2 = (x1_ref[rows, :]
              + jnp.dot(att_ref[rows, :], wout_ref[0:ATT_WIDTH, :], preferred_element_type=jnp.float32)
              + jnp.dot(pool_ref[rows, :], wout_ref[ATT_WIDTH:, :], preferred_element_type=jnp.float32))
        y_ref[rows, :] = x2
        xn[i] = _rms_norm(x2, g2_ref[...]).astype(jnp.bfloat16)

    def tail(i):
        rows = tiles[i]
        x3 = y_ref[rows, :] + 0.5 * acc_ref[rows, :]
        y_ref[rows, :] = _rms_norm(x3, gf_ref[...])

    def ffn(i, first, last):
        _swiglu_into(acc_ref.at[tiles[i]], xn[i], wg_ref, wu_ref, wd_ref, first, last)

    third = N_FF_CHUNKS // 3
    head(0)
    for i in range(len(tiles)):
        ffn(i, 0, third)
        if i > 0:
            tail(i - 1)
        ffn(i, third, 2 * third)
        if i + 1 < len(tiles):
            head(i + 1)
        ffn(i, 2 * third, N_FF_CHUNKS)
    tail(len(tiles) - 1)


def _resident(shape):
    zeros = (0,) * len(shape)
    return pl.BlockSpec(shape, lambda *_: zeros, pipeline_mode=pl.Buffered(1))


def _ffn_weights(w_gate, w_up, w_down):
    return w_gate.astype(jnp.bfloat16), w_up.astype(jnp.bfloat16), w_down.astype(jnp.bfloat16)


def _rope_tables(seq):
    half = DIFF_HEAD_DIM // 2
    pos = jnp.arange(seq, dtype=jnp.float32)
    inv_freq = 1.0 / (ROPE_THETA ** (jnp.arange(0, DIFF_HEAD_DIM, 2, dtype=jnp.float32) / DIFF_HEAD_DIM))
    ang = pos[:, None] * inv_freq[None, :]
    cos, sin = jnp.cos(ang), jnp.sin(ang)
    reps = LANES // half
    cos_t = jnp.tile(cos, (1, reps))
    sin_t = jnp.tile(jnp.concatenate([-sin, sin], axis=1), (1, reps // 2))
    return cos_t, sin_t


def kernel(x, ffn1_norm, ffn1_w_gate, ffn1_w_up, ffn1_w_down, mix_norm, w_in, lambda_q1, lambda_k1, lambda_q2, lambda_k2, subln_gain, pool_w, pool_scale, w_out, ffn2_norm, ffn2_w_gate, ffn2_w_up, ffn2_w_down, final_norm):
    B, S, D = x.shape
    assert D == D_MODEL and ffn1_norm.shape[0] == 1
    tm = min(TOKEN_TILE, S)
    tq = tk = tm
    assert S % tm == 0 and tq % CHUNK == 0 and tm >= POOL_HISTORY
    T = B * S
    n_tiles = T // tm
    tiles_per_seq = S // tm
    n_kblocks = n_qblocks = S // tk
    n_heads = N_DIFF_HEADS
    n_groups = ATT_GROUPS
    gsz = n_qblocks // n_groups
    slot_tab, kba_tab, kbb_tab, n_pairs_tab, n_tab = _attn_schedule(n_qblocks, n_groups)
    assert n_qblocks % n_groups == 0 and gsz % 4 == 0
    assert all(n % 2 == 0 for n in n_pairs_tab)
    f32, bf16 = jnp.float32, jnp.bfloat16

    xf = x.reshape(T, D)
    row = lambda a: a.reshape(1, -1).astype(f32)
    wg1, wu1, wd1 = _ffn_weights(ffn1_w_gate[0], ffn1_w_up[0], ffn1_w_down[0])
    wg2, wu2, wd2 = _ffn_weights(ffn2_w_gate[0], ffn2_w_up[0], ffn2_w_down[0])
    cos_t, sin_t = _rope_tables(S)
    params = pltpu.CompilerParams(dimension_semantics=("arbitrary",), vmem_limit_bytes=VMEM_LIMIT_BYTES)

    tok = lambda width: pl.BlockSpec((tm, width), lambda t: (t, 0))
    tab = pl.BlockSpec((tm, LANES), lambda t: (t % tiles_per_seq, 0))
    ffn_specs = [_resident((D, D_FF)), _resident((D, D_FF)), _resident((D_FF, D))]

    kt_spec = pl.BlockSpec((1, n_heads, 1, LANES, tk),
                           lambda t: (t // tiles_per_seq, 0, t % tiles_per_seq, 0, 0))
    x1, qz, kt, v, pool = pl.pallas_call(
        functools.partial(_pre_kernel, tiles_per_seq),
        grid=(n_tiles,),
        in_specs=[tok(D), _resident((1, D))] + ffn_specs
                 + [_resident((1, D)), _resident((D, IN_WIDTH)), tab, tab,
                    _resident((len(POOL_WINDOWS), POOL_GROUP_DIM, POOL_GROUP_DIM)),
                    _resident((1, POOL_WIDTH))],
        out_specs=[tok(D), pl.BlockSpec((2, tm, ATT_WIDTH), lambda t: (0, t, 0)), kt_spec,
                   tok(ATT_WIDTH), tok(POOL_WIDTH)],
        out_shape=[jax.ShapeDtypeStruct((T, D), f32),
                   jax.ShapeDtypeStruct((2, T, ATT_WIDTH), bf16),
                   jax.ShapeDtypeStruct((B, n_heads, n_kblocks, LANES, tk), bf16),
                   jax.ShapeDtypeStruct((T, ATT_WIDTH), bf16),
                   jax.ShapeDtypeStruct((T, POOL_WIDTH), bf16)],
        scratch_shapes=[pltpu.VMEM((tm, D), f32),
                        pltpu.VMEM((POOL_HISTORY + tm, POOL_WIDTH), f32)],
        compiler_params=params,
        name="pre_ffn_proj_pool",
    )(xf, row(ffn1_norm[0]), wg1, wu1, wd1, row(mix_norm[0]), w_in[0].astype(bf16), cos_t, sin_t,
      pool_w[0].astype(bf16), row(pool_scale[0]))

    lam_spec = pl.BlockSpec((1, DIFF_HEAD_DIM), lambda b, h, g, *_: (0, 0))
    att = pl.pallas_call(
        functools.partial(_attn_kernel, gsz, n_tab),
        grid_spec=pltpu.PrefetchScalarGridSpec(
            num_scalar_prefetch=4,
            grid=(B, n_heads, n_groups),
            in_specs=[pl.BlockSpec((2, 1, gsz * tq, LANES), lambda b, h, g, *_: (0, b, g, h)),
                      pl.BlockSpec((1, 1, n_kblocks, LANES, tk), lambda b, h, g, *_: (b, h, 0, 0, 0)),
                      pl.BlockSpec((1, S, LANES), lambda b, h, g, *_: (b, 0, h)),
                      lam_spec, lam_spec, lam_spec, lam_spec,
                      pl.BlockSpec((1, DIFF_V_DIM), lambda b, h, g, *_: (0, 0))],
            out_specs=pl.BlockSpec((1, gsz * tq, LANES), lambda b, h, g, *_: (b, g, h)),
            scratch_shapes=[pltpu.VMEM((2 * tq, 2 * tk), f32), pltpu.VMEM((2 * tq, 2 * tk), f32),
                            pltpu.VMEM((2 * tq, LANES), f32), pltpu.VMEM((2 * tq, LANES), f32),
                            pltpu.VMEM((gsz * tq, LANES), f32), pltpu.VMEM((gsz * tq, LANES), f32),
                            pltpu.VMEM((gsz * tq, 2 * LANES), f32),
                            pltpu.VMEM((gsz * tq, 2 * LANES), f32),
                            pltpu.VMEM((tq, tk), f32)]),
        out_shape=jax.ShapeDtypeStruct((B, S, ATT_WIDTH), bf16),
        compiler_params=pltpu.CompilerParams(
            dimension_semantics=("arbitrary", "arbitrary", "arbitrary"),
            vmem_limit_bytes=VMEM_LIMIT_BYTES),
        name="diff_flash_attention",
    )(jnp.asarray(slot_tab), jnp.asarray(kba_tab), jnp.asarray(kbb_tab), jnp.asarray(n_pairs_tab),
      qz.reshape(2, B, S, ATT_WIDTH), kt, v.reshape(B, S, ATT_WIDTH),
      row(lambda_q1[0]), row(lambda_k1[0]), row(lambda_q2[0]), row(lambda_k2[0]), row(subln_gain[0]))

    post_rows = POST_TILES_PER_STEP * tm
    assert T % post_rows == 0
    tok2 = lambda width: pl.BlockSpec((post_rows, width), lambda t: (t, 0))
    y = pl.pallas_call(
        functools.partial(_post_kernel, tm),
        grid=(T // post_rows,),
        in_specs=[tok2(D), tok2(ATT_WIDTH), tok2(POOL_WIDTH), _resident((D, D)), _resident((1, D))]
                 + ffn_specs + [_resident((1, D))],
        out_specs=tok2(D),
        out_shape=jax.ShapeDtypeStruct((T, D), f32),
        scratch_shapes=[pltpu.VMEM((post_rows, D), f32)],
        compiler_params=params,
        name="post_proj_ffn_norm",
    )(x1, att.reshape(T, ATT_WIDTH), pool, w_out[0].astype(bf16), row(ffn2_norm[0]),
      wg2, wu2, wd2, row(final_norm))
    return y.reshape(B, S, D)
```

```python
import functools
import math

import jax
import jax.numpy as jnp
import numpy as np
from jax import lax
from jax.experimental import pallas as pl
from jax.experimental.pallas import tpu as pltpu

D_MODEL = 1024
CHUNK = 64
ATT_WIDTH = 512
POOL_WIDTH = 512
N_DIFF_HEADS = 4
DIFF_HEAD_DIM = 64
DIFF_V_DIM = 128
POOL_WINDOWS = (2, 4, 8, 16)
POOL_GROUP_DIM = 128
D_FF = 2816
ROPE_THETA = 10000.0
NORM_EPS = 1e-6
IN_WIDTH = 2048
LAM_INIT = 0.8 - 0.6 * math.exp(-0.3 * 0)
Q_SCALE = DIFF_HEAD_DIM ** -0.5 * math.log2(math.e)

LANES = 128
SUBLANES = 8
VMEM_LIMIT_BYTES = 56 * 1024 * 1024

FF_CHUNK = 256
N_FF_CHUNKS = D_FF // FF_CHUNK
TOKEN_TILE = 512
POST_TILES_PER_STEP = 2
POOL_HISTORY = 16
ATT_BLOCK = TOKEN_TILE
ATT_GROUPS = 2


def _rms_norm(x, g):
    return x * lax.rsqrt(jnp.mean(x * x, axis=-1, keepdims=True) + NORM_EPS) * g


def _swiglu_into(acc_ref, xn, wg_ref, wu_ref, wd_ref, first=0, last=N_FF_CHUNKS):
    if first == 0:
        acc_ref[...] = jnp.zeros_like(acc_ref)
    for c in range(first, last):
        cols = slice(c * FF_CHUNK, (c + 1) * FF_CHUNK)
        g = jnp.dot(xn, wg_ref[:, cols], preferred_element_type=jnp.float32)
        u = jnp.dot(xn, wu_ref[:, cols], preferred_element_type=jnp.float32)
        a = (g * jax.nn.sigmoid(g) * u).astype(jnp.bfloat16)
        acc_ref[...] += jnp.dot(a, wd_ref[cols, :], preferred_element_type=jnp.float32)


def _rope_slab(t, cos, sin_signed):
    lane = lax.broadcasted_iota(jnp.int32, t.shape, 1)
    first_half = (lane % DIFF_HEAD_DIM) < (DIFF_HEAD_DIM // 2)
    partner = jnp.where(first_half,
                        pltpu.roll(t, LANES - DIFF_HEAD_DIM // 2, 1),
                        pltpu.roll(t, DIFF_HEAD_DIM // 2, 1))
    return t * cos + partner * sin_signed


def _pre_kernel(tiles_per_seq,
                x_ref, g1_ref, wg_ref, wu_ref, wd_ref, gmix_ref, win_ref, cos_ref, sin_ref,
                poolw_ref, pscale_ref,
                x1_ref, qz_ref, kt_ref, v_ref, pool_ref,
                acc_ref, ext_ref):
    tm = x_ref.shape[0]
    tile_in_seq = pl.program_id(0) % tiles_per_seq

    @pl.when(tile_in_seq == 0)
    def _():
        ext_ref[0:POOL_HISTORY, :] = jnp.zeros((POOL_HISTORY, POOL_WIDTH), jnp.float32)

    x = x_ref[...]
    xn = _rms_norm(x, g1_ref[...]).astype(jnp.bfloat16)
    _swiglu_into(acc_ref, xn, wg_ref, wu_ref, wd_ref)
    x1 = x + 0.5 * acc_ref[...]
    x1_ref[...] = x1

    h = _rms_norm(x1, gmix_ref[...]).astype(jnp.bfloat16)
    cos = cos_ref[...]
    sin = sin_ref[...]
    lane = lax.broadcasted_iota(jnp.int32, (tm, LANES), 1)
    zero = jnp.zeros((tm, LANES), jnp.bfloat16)

    ext_ref[POOL_HISTORY:, :] = jnp.dot(h, win_ref[:, 3 * ATT_WIDTH:],
                                        preferred_element_type=jnp.float32)
    pos = tile_in_seq * tm + lax.broadcasted_iota(jnp.int32, (tm, 1), 0)
    for gi, win in enumerate(POOL_WINDOWS):
        cols = slice(gi * POOL_GROUP_DIM, (gi + 1) * POOL_GROUP_DIM)
        ext = ext_ref[:, cols]
        wsum = ext
        shift = 1
        while shift < win:
            wsum = wsum + pltpu.roll(wsum, shift, 0)
            shift *= 2
        inv_cnt = 1.0 / jnp.minimum(pos + 1, win).astype(jnp.float32)
        d = (wsum[POOL_HISTORY:] * inv_cnt - ext[POOL_HISTORY:]).astype(jnp.bfloat16)
        y = jnp.dot(d, poolw_ref[gi], preferred_element_type=jnp.float32)
        pool_ref[:, cols] = (y * pscale_ref[:, cols]).astype(jnp.bfloat16)

        if gi < 3:
            pcols = slice(gi * ATT_WIDTH, (gi + 1) * ATT_WIDTH)
            proj = jnp.dot(h, win_ref[:, pcols], preferred_element_type=jnp.float32)
        if gi == 0:
            for hd in range(N_DIFF_HEADS):
                hcols = slice(hd * LANES, (hd + 1) * LANES)
                q = (_rope_slab(proj[:, hcols], cos, sin) * Q_SCALE).astype(jnp.bfloat16)
                qz_ref[0, :, hcols] = jnp.where(lane < DIFF_HEAD_DIM, q, zero)
                qz_ref[1, :, hcols] = jnp.where(lane >= DIFF_HEAD_DIM, q, zero)
        elif gi == 1:
            for hd in range(N_DIFF_HEADS):
                hcols = slice(hd * LANES, (hd + 1) * LANES)
                kt_ref[0, hd, 0] = _rope_slab(proj[:, hcols], cos, sin).T.astype(jnp.bfloat16)
        elif gi == 2:
            v_ref[...] = proj.astype(jnp.bfloat16)
    ext_ref[0:POOL_HISTORY, :] = ext_ref[tm:tm + POOL_HISTORY, :]


def _attn_schedule(n_qblocks, n_groups):
    gsz = n_qblocks // n_groups
    rows, n_pairs = [], []
    for g in range(n_groups):
        qblocks = range(g * gsz, (g + 1) * gsz)
        steps = [(i - g * gsz, 2 * c, 2 * c + 1) for i in qblocks for c in range(i // 2)]
        n_pairs.append(len(steps))
        steps += [(i - g * gsz, i - 1, i) for i in qblocks if i % 2 == 1]
        steps += [(i - g * gsz, i, min(i + 1, n_qblocks - 1)) for i in qblocks if i % 2 == 0]
        steps.append(steps[-1])
        rows.append(steps)
    n_tab = max(len(r) for r in rows)
    rows = [r + [r[-1]] * (n_tab - len(r)) for r in rows]
    column = lambda k: np.array([[st[k] for st in r] for r in rows], np.int32).reshape(-1)
    return column(0), column(1), column(2), np.array(n_pairs, np.int32), n_tab


def _attn_kernel(gsz, n_tab,
                 slot_tab, kba_tab, kbb_tab, n_pairs_tab,
                 qz_ref, kt_ref, v_ref, lq1_ref, lk1_ref, lq2_ref, lk2_ref, gain_ref,
                 o_ref, s_a, s_b, mx_a, mx_b, m1_ref, m2_ref, acc1_ref, acc2_ref, bias_ref):
    tq = s_a.shape[0] // 2
    tk = s_a.shape[1] // 2
    g = pl.program_id(2)
    s_bufs, mx_bufs = (s_a, s_b), (mx_a, mx_b)
    map_rows = (slice(0, tq), slice(tq, 2 * tq))
    n_pairs = n_pairs_tab[g]

    def entry(tab, t):
        return tab[g * n_tab + t]

    m_refs, acc_refs = (m1_ref, m2_ref), (acc1_ref, acc2_ref)

    def state_rows(slot):
        return pl.ds(pl.multiple_of(slot * tq, tq), tq)

    @pl.when((pl.program_id(0) == 0) & (pl.program_id(1) == 0) & (g == 0))
    def _():
        for m_ref, acc_ref in zip(m_refs, acc_refs):
            m_ref[...] = jnp.full(m_ref.shape, -jnp.inf, jnp.float32)
            acc_ref[...] = jnp.zeros_like(acc_ref)
        row_chunk = lax.broadcasted_iota(jnp.int32, bias_ref.shape, 0) // CHUNK
        col_chunk = lax.broadcasted_iota(jnp.int32, bias_ref.shape, 1) // CHUNK
        bias_ref[...] = jnp.where(col_chunk <= row_chunk, 0.0, -jnp.inf)

    def qk(t, par, n_blocks, with_max, maps=(0, 1)):
        kt = kt_ref[0, 0, entry(kba_tab, t)]
        if n_blocks == 2:
            kt = jnp.concatenate([kt, kt_ref[0, 0, entry(kbb_tab, t)]], axis=1)
        qrows = state_rows(entry(slot_tab, t))
        for mp in maps:
            rows = map_rows[mp]
            s = jnp.dot(qz_ref[mp, 0, qrows, :], kt, preferred_element_type=jnp.float32)
            s_bufs[par][rows, 0:n_blocks * tk] = s
            if with_max:
                mx_bufs[par][rows, :] = functools.reduce(
                    jnp.maximum, [s[:, c * LANES:(c + 1) * LANES] for c in range(s.shape[1] // LANES)])

    def n_keys_of(kind):
        return tk if kind == "diag" else 2 * tk

    def problem(kind, mp, half):
        n_keys = n_keys_of(kind)
        if half is None:
            return mp * tq, 0, tq, n_keys
        nrows = tq // 2
        return mp * tq + half * nrows, half * nrows, nrows, n_keys - (tk // 2 if half == 0 else 0)

    def softmax(t, par, kind, mp, half=None):
        buf_row, blk_row, nrows, n_keys = problem(kind, mp, half)
        rows = slice(buf_row, buf_row + nrows)
        s = s_bufs[par][rows, 0:n_keys]
        slabs = [s[:, c * LANES:(c + 1) * LANES] for c in range(n_keys // LANES)]
        if kind == "pair":
            mx = mx_bufs[par][rows, :]
        else:
            first = (n_keys_of(kind) - tk) // LANES
            slabs[first:] = [sl + bias_ref[blk_row:blk_row + nrows, c * LANES:(c + 1) * LANES]
                             for c, sl in enumerate(slabs[first:])]
            mx = functools.reduce(jnp.maximum, slabs)
        srows = pl.ds(pl.multiple_of(entry(slot_tab, t) * tq, tq) + blk_row, nrows)
        m_old = m_refs[mp][srows, :]
        m_new = jnp.maximum(m_old, jnp.max(mx, axis=1, keepdims=True))
        alpha = jnp.exp2(m_old - m_new)
        p = jnp.concatenate([jnp.exp2(sl - m_new) for sl in slabs], axis=1).astype(jnp.bfloat16)
        return p, alpha, m_new

    def pv_acc(t, kind, mp, stats, half=None):
        p, alpha, m_new = stats
        _, blk_row, nrows, n_keys = problem(kind, mp, half)
        start = pl.multiple_of(entry(kba_tab, t) * tk, tk)
        vext = jnp.concatenate([v_ref[0, pl.ds(start, n_keys), :],
                                jnp.ones((n_keys, LANES), jnp.bfloat16)], axis=1)
        srows = pl.ds(pl.multiple_of(entry(slot_tab, t) * tq, tq) + blk_row, nrows)
        m_ref, acc_ref = m_refs[mp], acc_refs[mp]
        acc_ref[srows, :] = (acc_ref[srows, :] * jnp.concatenate([alpha, alpha], axis=1)
                             + jnp.dot(p, vext, preferred_element_type=jnp.float32))
        m_ref[srows, :] = m_new

    def finalize(t):
        srows = state_rows(entry(slot_tab, t))
        lam = (jnp.exp(jnp.sum(lq1_ref[...] * lk1_ref[...], axis=1, keepdims=True))
               - jnp.exp(jnp.sum(lq2_ref[...] * lk2_ref[...], axis=1, keepdims=True))
               + LAM_INIT)
        a = []
        for m_ref, acc_ref in zip(m_refs, acc_refs):
            acc = acc_ref[srows, :]
            a.append(acc[:, :DIFF_V_DIM] / acc[:, DIFF_V_DIM:])
            acc_ref[srows, :] = jnp.zeros((tq, 2 * LANES), jnp.float32)
            m_ref[srows, :] = jnp.full((tq, LANES), -jnp.inf, jnp.float32)
        o = _rms_norm(a[0] - lam * a[1], gain_ref[...]) * (1.0 - LAM_INIT)
        o_ref[0, srows, :] = o.astype(o_ref.dtype)

    def run(first_step, n_steps, kind, steps_per_trip=2):
        n_blocks = 1 if kind == "diag" else 2
        with_max = kind == "pair"

        def trip(c, carry):
            t = first_step + steps_per_trip * c
            for k in range(steps_per_trip):
                cur, nxt = k % 2, (k + 1) % 2
                for mp in (0, 1):
                    qk(t + k + 1, nxt, n_blocks, with_max, (mp,))
                    for half in ((None,) if kind == "pair" else (0, 1)):
                        pv_acc(t + k, kind, mp, softmax(t + k, cur, kind, mp, half), half)
                if kind != "pair":
                    finalize(t + k)
            return carry

        lax.fori_loop(0, n_steps // steps_per_trip, trip, 0)

    n_half = gsz // 2
    qk(0, 0, 2, True)
    run(0, n_pairs, "pair")
    run(n_pairs, n_half, "edge")
    run(n_pairs + n_half, n_half, "diag")


def _post_kernel(tm, x1_ref, att_ref, pool_ref, wout_ref, g2_ref, wg_ref, wu_ref, wd_ref, gf_ref,
                 y_ref, acc_ref):
    tiles = [pl.ds(r0, tm) for r0 in range(0, x1_ref.shape[0], tm)]
    xn = {}

    def head(i):
        rows = tiles[i]
        x2 = (x1_ref[rows, :]
              + jnp.dot(att_ref[rows, :], wout_ref[0:ATT_WIDTH, :], preferred_element_type=jnp.float32)
              + jnp.dot(pool_ref[rows, :], wout_ref[ATT_WIDTH:, :], preferred_element_type=jnp.float32))
        y_ref[rows, :] = x2
        xn[i] = _rms_norm(x2, g2_ref[...]).astype(jnp.bfloat16)

    def tail(i):
        rows = tiles[i]
        x3 = y_ref[rows, :] + 0.5 * acc_ref[rows, :]
        y_ref[rows, :] = _rms_norm(x3, gf_ref[...])

    def ffn(i, first, last):
        _swiglu_into(acc_ref.at[tiles[i]], xn[i], wg_ref, wu_ref, wd_ref, first, last)

    third = N_FF_CHUNKS // 3
    head(0)
    for i in range(len(tiles)):
        ffn(i, 0, third)
        if i > 0:
            tail(i - 1)
        ffn(i, third, 2 * third)
        if i + 1 < len(tiles):
            head(i + 1)
        ffn(i, 2 * third, N_FF_CHUNKS)
    tail(len(tiles) - 1)


def _resident(shape):
    zeros = (0,) * len(shape)
    return pl.BlockSpec(shape, lambda *_: zeros, pipeline_mode=pl.Buffered(1))


def _ffn_weights(w_gate, w_up, w_down):
    return w_gate.astype(jnp.bfloat16), w_up.astype(jnp.bfloat16), w_down.astype(jnp.bfloat16)


def _rope_tables(seq):
    half = DIFF_HEAD_DIM // 2
    pos = jnp.arange(seq, dtype=jnp.float32)
    inv_freq = 1.0 / (ROPE_THETA ** (jnp.arange(0, DIFF_HEAD_DIM, 2, dtype=jnp.float32) / DIFF_HEAD_DIM))
    ang = pos[:, None] * inv_freq[None, :]
    cos, sin = jnp.cos(ang), jnp.sin(ang)
    reps = LANES // half
    cos_t = jnp.tile(cos, (1, reps))
    sin_t = jnp.tile(jnp.concatenate([-sin, sin], axis=1), (1, reps // 2))
    return cos_t, sin_t


def kernel(x, ffn1_norm, ffn1_w_gate, ffn1_w_up, ffn1_w_down, mix_norm, w_in, lambda_q1, lambda_k1, lambda_q2, lambda_k2, subln_gain, pool_w, pool_scale, w_out, ffn2_norm, ffn2_w_gate, ffn2_w_up, ffn2_w_down, final_norm):
    B, S, D = x.shape
    assert D == D_MODEL and ffn1_norm.shape[0] == 1
    tm = min(TOKEN_TILE, S)
    tq = tk = tm
    assert S % tm == 0 and tq % CHUNK == 0 and tm >= POOL_HISTORY
    T = B * S
    n_tiles = T // tm
    tiles_per_seq = S // tm
    n_kblocks = n_qblocks = S // tk
    n_heads = N_DIFF_HEADS
    n_groups = ATT_GROUPS
    gsz = n_qblocks // n_groups
    slot_tab, kba_tab, kbb_tab, n_pairs_tab, n_tab = _attn_schedule(n_qblocks, n_groups)
    assert n_qblocks % n_groups == 0 and gsz % 4 == 0
    assert all(n % 2 == 0 for n in n_pairs_tab)
    f32, bf16 = jnp.float32, jnp.bfloat16

    xf = x.reshape(T, D)
    row = lambda a: a.reshape(1, -1).astype(f32)
    wg1, wu1, wd1 = _ffn_weights(ffn1_w_gate[0], ffn1_w_up[0], ffn1_w_down[0])
    wg2, wu2, wd2 = _ffn_weights(ffn2_w_gate[0], ffn2_w_up[0], ffn2_w_down[0])
    cos_t, sin_t = _rope_tables(S)
    params = pltpu.CompilerParams(dimension_semantics=("arbitrary",), vmem_limit_bytes=VMEM_LIMIT_BYTES)

    tok = lambda width: pl.BlockSpec((tm, width), lambda t: (t, 0))
    tab = pl.BlockSpec((tm, LANES), lambda t: (t % tiles_per_seq, 0))
    ffn_specs = [_resident((D, D_FF)), _resident((D, D_FF)), _resident((D_FF, D))]

    kt_spec = pl.BlockSpec((1, n_heads, 1, LANES, tk),
                           lambda t: (t // tiles_per_seq, 0, t % tiles_per_seq, 0, 0))
    x1, qz, kt, v, pool = pl.pallas_call(
        functools.partial(_pre_kernel, tiles_per_seq),
        grid=(n_tiles,),
        in_specs=[tok(D), _resident((1, D))] + ffn_specs
                 + [_resident((1, D)), _resident((D, IN_WIDTH)), tab, tab,
                    _resident((len(POOL_WINDOWS), POOL_GROUP_DIM, POOL_GROUP_DIM)),
                    _resident((1, POOL_WIDTH))],
        out_specs=[tok(D), pl.BlockSpec((2, tm, ATT_WIDTH), lambda t: (0, t, 0)), kt_spec,
                   tok(ATT_WIDTH), tok(POOL_WIDTH)],
        out_shape=[jax.ShapeDtypeStruct((T, D), f32),
                   jax.ShapeDtypeStruct((2, T, ATT_WIDTH), bf16),
                   jax.ShapeDtypeStruct((B, n_heads, n_kblocks, LANES, tk), bf16),
                   jax.ShapeDtypeStruct((T, ATT_WIDTH), bf16),
                   jax.ShapeDtypeStruct((T, POOL_WIDTH), bf16)],
        scratch_shapes=[pltpu.VMEM((tm, D), f32),
                        pltpu.VMEM((POOL_HISTORY + tm, POOL_WIDTH), f32)],
        compiler_params=params,
        name="pre_ffn_proj_pool",
    )(xf, row(ffn1_norm[0]), wg1, wu1, wd1, row(mix_norm[0]), w_in[0].astype(bf16), cos_t, sin_t,
      pool_w[0].astype(bf16), row(pool_scale[0]))

    lam_spec = pl.BlockSpec((1, DIFF_HEAD_DIM), lambda b, h, g, *_: (0, 0))
    att = pl.pallas_call(
        functools.partial(_attn_kernel, gsz, n_tab),
        grid_spec=pltpu.PrefetchScalarGridSpec(
            num_scalar_prefetch=4,
            grid=(B, n_heads, n_groups),
            in_specs=[pl.BlockSpec((2, 1, gsz * tq, LANES), lambda b, h, g, *_: (0, b, g, h)),
                      pl.BlockSpec((1, 1, n_kblocks, LANES, tk), lambda b, h, g, *_: (b, h, 0, 0, 0)),
                      pl.BlockSpec((1, S, LANES), lambda b, h, g, *_: (b, 0, h)),
                      lam_spec, lam_spec, lam_spec, lam_spec,
                      pl.BlockSpec((1, DIFF_V_DIM), lambda b, h, g, *_: (0, 0))],
            out_specs=pl.BlockSpec((1, gsz * tq, LANES), lambda b, h, g, *_: (b, g, h)),
            scratch_shapes=[pltpu.VMEM((2 * tq, 2 * tk), f32), pltpu.VMEM((2 * tq, 2 * tk), f32),
                            pltpu.VMEM((2 * tq, LANES), f32), pltpu.VMEM((2 * tq, LANES), f32),
                            pltpu.VMEM((gsz * tq, LANES), f32), pltpu.VMEM((gsz * tq, LANES), f32),
                            pltpu.VMEM((gsz * tq, 2 * LANES), f32),
                            pltpu.VMEM((gsz * tq, 2 * LANES), f32),
                            pltpu.VMEM((tq, tk), f32)]),
        out_shape=jax.ShapeDtypeStruct((B, S, ATT_WIDTH), bf16),
        compiler_params=pltpu.CompilerParams(
            dimension_semantics=("arbitrary", "arbitrary", "arbitrary"),
            vmem_limit_bytes=VMEM_LIMIT_BYTES),
        name="diff_flash_attention",
    )(jnp.asarray(slot_tab), jnp.asarray(kba_tab), jnp.asarray(kbb_tab), jnp.asarray(n_pairs_tab),
      qz.reshape(2, B, S, ATT_WIDTH), kt, v.reshape(B, S, ATT_WIDTH),
      row(lambda_q1[0]), row(lambda_k1[0]), row(lambda_q2[0]), row(lambda_k2[0]), row(subln_gain[0]))

    post_rows = POST_TILES_PER_STEP * tm
    assert T % post_rows == 0
    tok2 = lambda width: pl.BlockSpec((post_rows, width), lambda t: (t, 0))
    y = pl.pallas_call(
        functools.partial(_post_kernel, tm),
        grid=(T // post_rows,),
        in_specs=[tok2(D), tok2(ATT_WIDTH), tok2(POOL_WIDTH), _resident((D, D)), _resident((1, D))]
                 + ffn_specs + [_resident((1, D))],
        out_specs=tok2(D),
        out_shape=jax.ShapeDtypeStruct((T, D), f32),
        scratch_shapes=[pltpu.VMEM((post_rows, D), f32)],
        compiler_params=params,
        name="post_proj_ffn_norm",
    )(x1, att.reshape(T, ATT_WIDTH), pool, w_out[0].astype(bf16), row(ffn2_norm[0]),
      wg2, wu2, wd2, row(final_norm))
    return y.reshape(B, S, D)
```

```python
import functools
import math

import jax
import jax.numpy as jnp
import numpy as np
from jax import lax
from jax.experimental import pallas as pl
from jax.experimental.pallas import tpu as pltpu

D_MODEL = 1024
CHUNK = 64
ATT_WIDTH = 512
POOL_WIDTH = 512
N_DIFF_HEADS = 4
DIFF_HEAD_DIM = 64
DIFF_V_DIM = 128
POOL_WINDOWS = (2, 4, 8, 16)
POOL_GROUP_DIM = 128
D_FF = 2816
ROPE_THETA = 10000.0
NORM_EPS = 1e-6
IN_WIDTH = 2048
LAM_INIT = 0.8 - 0.6 * math.exp(-0.3 * 0)
Q_SCALE = DIFF_HEAD_DIM ** -0.5 * math.log2(math.e)

LANES = 128
SUBLANES = 8
VMEM_LIMIT_BYTES = 56 * 1024 * 1024

FF_CHUNK = 256
N_FF_CHUNKS = D_FF // FF_CHUNK
TOKEN_TILE = 512
POST_TILES_PER_STEP = 2
POOL_HISTORY = 16
ATT_BLOCK = TOKEN_TILE
ATT_GROUPS = 2


def _rms_norm(x, g):
    return x * lax.rsqrt(jnp.mean(x * x, axis=-1, keepdims=True) + NORM_EPS) * g


def _swiglu_into(acc_ref, xn, wg_ref, wu_ref, wd_ref, first=0, last=N_FF_CHUNKS):
    if first == 0:
        acc_ref[...] = jnp.zeros_like(acc_ref)
    for c in range(first, last):
        cols = slice(c * FF_CHUNK, (c + 1) * FF_CHUNK)
        g = jnp.dot(xn, wg_ref[:, cols], preferred_element_type=jnp.float32)
        u = jnp.dot(xn, wu_ref[:, cols], preferred_element_type=jnp.float32)
        a = (g * jax.nn.sigmoid(g) * u).astype(jnp.bfloat16)
        acc_ref[...] += jnp.dot(a, wd_ref[cols, :], preferred_element_type=jnp.float32)


def _rope_slab(t, cos, sin_signed):
    lane = lax.broadcasted_iota(jnp.int32, t.shape, 1)
    first_half = (lane % DIFF_HEAD_DIM) < (DIFF_HEAD_DIM // 2)
    partner = jnp.where(first_half,
                        pltpu.roll(t, LANES - DIFF_HEAD_DIM // 2, 1),
                        pltpu.roll(t, DIFF_HEAD_DIM // 2, 1))
    return t * cos + partner * sin_signed


def _pre_kernel(tiles_per_seq,
                x_ref, g1_ref, wg_ref, wu_ref, wd_ref, gmix_ref, win_ref, cos_ref, sin_ref,
                poolw_ref, pscale_ref,
                x1_ref, qz_ref, kt_ref, v_ref, pool_ref,
                acc_ref, ext_ref):
    tm = x_ref.shape[0]
    tile_in_seq = pl.program_id(0) % tiles_per_seq

    @pl.when(tile_in_seq == 0)
    def _():
        ext_ref[0:POOL_HISTORY, :] = jnp.zeros((POOL_HISTORY, POOL_WIDTH), jnp.float32)

    x = x_ref[...]
    xn = _rms_norm(x, g1_ref[...]).astype(jnp.bfloat16)
    _swiglu_into(acc_ref, xn, wg_ref, wu_ref, wd_ref)
    x1 = x + 0.5 * acc_ref[...]
    x1_ref[...] = x1

    h = _rms_norm(x1, gmix_ref[...]).astype(jnp.bfloat16)
    cos = cos_ref[...]
    sin = sin_ref[...]
    lane = lax.broadcasted_iota(jnp.int32, (tm, LANES), 1)
    zero = jnp.zeros((tm, LANES), jnp.bfloat16)

    ext_ref[POOL_HISTORY:, :] = jnp.dot(h, win_ref[:, 3 * ATT_WIDTH:],
                                        preferred_element_type=jnp.float32)
    pos = tile_in_seq * tm + lax.broadcasted_iota(jnp.int32, (tm, 1), 0)
    for gi, win in enumerate(POOL_WINDOWS):
        cols = slice(gi * POOL_GROUP_DIM, (gi + 1) * POOL_GROUP_DIM)
        ext = ext_ref[:, cols]
        wsum = ext
        shift = 1
        while shift < win:
            wsum = wsum + pltpu.roll(wsum, shift, 0)
            shift *= 2
        inv_cnt = 1.0 / jnp.minimum(pos + 1, win).astype(jnp.float32)
        d = (wsum[POOL_HISTORY:] * inv_cnt - ext[POOL_HISTORY:]).astype(jnp.bfloat16)
        if gi % 2 == 0:
            d_even = d
        else:
            wz = jnp.zeros((POOL_GROUP_DIM, POOL_GROUP_DIM), jnp.bfloat16)
            w2 = jnp.concatenate([jnp.concatenate([poolw_ref[gi - 1], wz], axis=1),
                                  jnp.concatenate([wz, poolw_ref[gi]], axis=1)], axis=0)
            y = jnp.dot(jnp.concatenate([d_even, d], axis=1), w2, preferred_element_type=jnp.float32)
            cols2 = slice((gi - 1) * POOL_GROUP_DIM, (gi + 1) * POOL_GROUP_DIM)
            pool_ref[:, cols2] = (y * pscale_ref[:, cols2]).astype(jnp.bfloat16)

        if gi < 3:
            pcols = slice(gi * ATT_WIDTH, (gi + 1) * ATT_WIDTH)
            proj = jnp.dot(h, win_ref[:, pcols], preferred_element_type=jnp.float32)
        if gi == 0:
            for hd in range(N_DIFF_HEADS):
                hcols = slice(hd * LANES, (hd + 1) * LANES)
                q = (_rope_slab(proj[:, hcols], cos, sin) * Q_SCALE).astype(jnp.bfloat16)
                qz_ref[0, :, hcols] = jnp.where(lane < DIFF_HEAD_DIM, q, zero)
                qz_ref[1, :, hcols] = jnp.where(lane >= DIFF_HEAD_DIM, q, zero)
        elif gi == 1:
            for hd in range(N_DIFF_HEADS):
                hcols = slice(hd * LANES, (hd + 1) * LANES)
                kt_ref[0, hd, 0] = _rope_slab(proj[:, hcols], cos, sin).T.astype(jnp.bfloat16)
        elif gi == 2:
            v_ref[...] = proj.astype(jnp.bfloat16)
    ext_ref[0:POOL_HISTORY, :] = ext_ref[tm:tm + POOL_HISTORY, :]


def _attn_schedule(n_qblocks, n_groups):
    gsz = n_qblocks // n_groups
    rows, n_pairs = [], []
    for g in range(n_groups):
        qblocks = range(g * gsz, (g + 1) * gsz)
        steps = [(i - g * gsz, 2 * c, 2 * c + 1) for i in qblocks for c in range(i // 2)]
        n_pairs.append(len(steps))
        steps += [(i - g * gsz, i - 1, i) for i in qblocks if i % 2 == 1]
        steps += [(i - g * gsz, i, min(i + 1, n_qblocks - 1)) for i in qblocks if i % 2 == 0]
        steps.append(steps[-1])
        rows.append(steps)
    n_tab = max(len(r) for r in rows)
    rows = [r + [r[-1]] * (n_tab - len(r)) for r in rows]
    column = lambda k: np.array([[st[k] for st in r] for r in rows], np.int32).reshape(-1)
    return column(0), column(1), column(2), np.array(n_pairs + [gsz // 2], np.int32), n_tab


def _attn_kernel(gsz, n_tab,
                 slot_tab, kba_tab, kbb_tab, n_pairs_tab,
                 qz_ref, kt_ref, v_ref, lq1_ref, lk1_ref, lq2_ref, lk2_ref, gain_ref,
                 o_ref, s_a, s_b, mx_a, mx_b, m1_ref, m2_ref, acc1_ref, acc2_ref, bias_ref):
    tq = s_a.shape[0] // 2
    tk = s_a.shape[1] // 2
    g = pl.program_id(2)
    s_bufs, mx_bufs = (s_a, s_b), (mx_a, mx_b)
    map_rows = (slice(0, tq), slice(tq, 2 * tq))
    n_pairs = n_pairs_tab[g]

    def entry(tab, t):
        return tab[g * n_tab + t]

    m_refs, acc_refs = (m1_ref, m2_ref), (acc1_ref, acc2_ref)

    def state_rows(slot):
        return pl.ds(pl.multiple_of(slot * tq, tq), tq)

    @pl.when((pl.program_id(0) == 0) & (pl.program_id(1) == 0) & (g == 0))
    def _():
        for m_ref, acc_ref in zip(m_refs, acc_refs):
            m_ref[...] = jnp.full(m_ref.shape, -jnp.inf, jnp.float32)
            acc_ref[...] = jnp.zeros_like(acc_ref)
        row_chunk = lax.broadcasted_iota(jnp.int32, bias_ref.shape, 0) // CHUNK
        col_chunk = lax.broadcasted_iota(jnp.int32, bias_ref.shape, 1) // CHUNK
        bias_ref[...] = jnp.where(col_chunk <= row_chunk, 0.0, -jnp.inf)

    def qk(t, par, n_blocks, with_max, maps=(0, 1)):
        kt = kt_ref[0, 0, entry(kba_tab, t)]
        if n_blocks == 2:
            kt = jnp.concatenate([kt, kt_ref[0, 0, entry(kbb_tab, t)]], axis=1)
        qrows = state_rows(entry(slot_tab, t))
        for mp in maps:
            rows = map_rows[mp]
            s = jnp.dot(qz_ref[mp, 0, qrows, :], kt, preferred_element_type=jnp.float32)
            s_bufs[par][rows, 0:n_blocks * tk] = s
            if with_max:
                mx_bufs[par][rows, :] = functools.reduce(
                    jnp.maximum, [s[:, c * LANES:(c + 1) * LANES] for c in range(s.shape[1] // LANES)])

    def n_keys_of(kind):
        return tk if kind == "diag" else 2 * tk

    def problem(kind, mp, half):
        n_keys = n_keys_of(kind)
        if half is None:
            return mp * tq, 0, tq, n_keys
        nrows = tq // 2
        return mp * tq + half * nrows, half * nrows, nrows, n_keys - (tk // 2 if half == 0 else 0)

    def softmax(t, par, kind, mp, half=None):
        buf_row, blk_row, nrows, n_keys = problem(kind, mp, half)
        rows = slice(buf_row, buf_row + nrows)
        s = s_bufs[par][rows, 0:n_keys]
        slabs = [s[:, c * LANES:(c + 1) * LANES] for c in range(n_keys // LANES)]
        if kind == "pair":
            mx = mx_bufs[par][rows, :]
        else:
            first = (n_keys_of(kind) - tk) // LANES
            slabs[first:] = [sl + bias_ref[blk_row:blk_row + nrows, c * LANES:(c + 1) * LANES]
                             for c, sl in enumerate(slabs[first:])]
            mx = functools.reduce(jnp.maximum, slabs)
        srows = pl.ds(pl.multiple_of(entry(slot_tab, t) * tq, tq) + blk_row, nrows)
        m_old = m_refs[mp][srows, :]
        m_new = jnp.maximum(m_old, jnp.max(mx, axis=1, keepdims=True))
        alpha = jnp.exp2(m_old - m_new)
        p = jnp.concatenate([jnp.exp2(sl - m_new) for sl in slabs], axis=1).astype(jnp.bfloat16)
        return p, alpha, m_new

    def pv_acc(t, kind, mp, stats, half=None):
        p, alpha, m_new = stats
        _, blk_row, nrows, n_keys = problem(kind, mp, half)
        start = pl.multiple_of(entry(kba_tab, t) * tk, tk)
        vext = jnp.concatenate([v_ref[0, pl.ds(start, n_keys), :],
                                jnp.ones((n_keys, LANES), jnp.bfloat16)], axis=1)
        srows = pl.ds(pl.multiple_of(entry(slot_tab, t) * tq, tq) + blk_row, nrows)
        m_ref, acc_ref = m_refs[mp], acc_refs[mp]
        acc_ref[srows, :] = (acc_ref[srows, :] * jnp.concatenate([alpha, alpha], axis=1)
                             + jnp.dot(p, vext, preferred_element_type=jnp.float32))
        m_ref[srows, :] = m_new

    def finalize(t):
        srows = state_rows(entry(slot_tab, t))
        lam = (jnp.exp(jnp.sum(lq1_ref[...] * lk1_ref[...], axis=1, keepdims=True))
               - jnp.exp(jnp.sum(lq2_ref[...] * lk2_ref[...], axis=1, keepdims=True))
               + LAM_INIT)
        a = []
        for m_ref, acc_ref in zip(m_refs, acc_refs):
            acc = acc_ref[srows, :]
            a.append(acc[:, :DIFF_V_DIM] / acc[:, DIFF_V_DIM:])
            acc_ref[srows, :] = jnp.zeros((tq, 2 * LANES), jnp.float32)
            m_ref[srows, :] = jnp.full((tq, LANES), -jnp.inf, jnp.float32)
        o = _rms_norm(a[0] - lam * a[1], gain_ref[...]) * (1.0 - LAM_INIT)
        o_ref[0, srows, :] = o.astype(o_ref.dtype)

    def run(first_step, n_steps, kind, steps_per_trip=2):
        n_blocks = 1 if kind == "diag" else 2
        with_max = kind == "pair"

        def trip(c, carry):
            t = first_step + steps_per_trip * c
            for k in range(steps_per_trip):
                cur, nxt = k % 2, (k + 1) % 2
                for mp in (0, 1):
                    qk(t + k + 1, nxt, n_blocks, with_max, (mp,))
                    for half in ((None,) if kind == "pair" else (0, 1)):
                        pv_acc(t + k, kind, mp, softmax(t + k, cur, kind, mp, half), half)
                if kind != "pair":
                    finalize(t + k)
            return carry

        lax.fori_loop(0, n_steps // steps_per_trip, trip, 0)

    n_half = n_pairs_tab[pl.num_programs(2)]
    masked_steps_per_trip = 4 if (gsz // 2) % 4 == 0 else 2
    qk(0, 0, 2, True)
    run(0, n_pairs, "pair")
    run(n_pairs, n_half, "edge", masked_steps_per_trip)
    run(n_pairs + n_half, n_half, "diag", masked_steps_per_trip)


def _post_kernel(tm, x1_ref, att_ref, pool_ref, wout_ref, g2_ref, wg_ref, wu_ref, wd_ref, gf_ref,
                 y_ref, acc_ref):
    tiles = [pl.ds(r0, tm) for r0 in range(0, x1_ref.shape[0], tm)]
    xn = {}

    def head(i):
        rows = tiles[i]
        x2 = (x1_ref[rows, :]
              + jnp.dot(att_ref[rows, :], wout_ref[0:ATT_WIDTH, :], preferred_element_type=jnp.float32)
              + jnp.dot(pool_ref[rows, :], wout_ref[ATT_WIDTH:, :], preferred_element_type=jnp.float32))
        y_ref[rows, :] = x2
        xn[i] = _rms_norm(x2, g2_ref[...]).astype(jnp.bfloat16)

    def tail(i):
        rows = tiles[i]
        x3 = y_ref[rows, :] + 0.5 * acc_ref[rows, :]
        y_ref[rows, :] = _rms_norm(x3, gf_ref[...])

    def ffn(i, first, last):
        _swiglu_into(acc_ref.at[tiles[i]], xn[i], wg_ref, wu_ref, wd_ref, first, last)

    third = N_FF_CHUNKS // 3
    head(0)
    for i in range(len(tiles)):
        ffn(i, 0, third)
        if i > 0:
            tail(i - 1)
        ffn(i, third, 2 * third)
        if i + 1 < len(tiles):
            head(i + 1)
        ffn(i, 2 * third, N_FF_CHUNKS)
    tail(len(tiles) - 1)


def _resident(shape):
    zeros = (0,) * len(shape)
    return pl.BlockSpec(shape, lambda *_: zeros, pipeline_mode=pl.Buffered(1))


def _ffn_weights(w_gate, w_up, w_down):
    return w_gate.astype(jnp.bfloat16), w_up.astype(jnp.bfloat16), w_down.astype(jnp.bfloat16)


def _rope_tables(seq):
    half = DIFF_HEAD_DIM // 2
    pos = jnp.arange(seq, dtype=jnp.float32)
    inv_freq = 1.0 / (ROPE_THETA ** (jnp.arange(0, DIFF_HEAD_DIM, 2, dtype=jnp.float32) / DIFF_HEAD_DIM))
    ang = pos[:, None] * inv_freq[None, :]
    cos, sin = jnp.cos(ang), jnp.sin(ang)
    reps = LANES // half
    cos_t = jnp.tile(cos, (1, reps))
    sin_t = jnp.tile(jnp.concatenate([-sin, sin], axis=1), (1, reps // 2))
    return cos_t, sin_t


def kernel(x, ffn1_norm, ffn1_w_gate, ffn1_w_up, ffn1_w_down, mix_norm, w_in, lambda_q1, lambda_k1, lambda_q2, lambda_k2, subln_gain, pool_w, pool_scale, w_out, ffn2_norm, ffn2_w_gate, ffn2_w_up, ffn2_w_down, final_norm):
    B, S, D = x.shape
    assert D == D_MODEL and ffn1_norm.shape[0] == 1
    tm = min(TOKEN_TILE, S)
    tq = tk = tm
    assert S % tm == 0 and tq % CHUNK == 0 and tm >= POOL_HISTORY
    T = B * S
    n_tiles = T // tm
    tiles_per_seq = S // tm
    n_kblocks = n_qblocks = S // tk
    n_heads = N_DIFF_HEADS
    n_groups = ATT_GROUPS
    gsz = n_qblocks // n_groups
    slot_tab, kba_tab, kbb_tab, n_pairs_tab, n_tab = _attn_schedule(n_qblocks, n_groups)
    assert n_qblocks % n_groups == 0 and gsz % 4 == 0
    assert all(n % 2 == 0 for n in n_pairs_tab)
    f32, bf16 = jnp.float32, jnp.bfloat16

    xf = x.reshape(T, D)
    row = lambda a: a.reshape(1, -1).astype(f32)
    wg1, wu1, wd1 = _ffn_weights(ffn1_w_gate[0], ffn1_w_up[0], ffn1_w_down[0])
    wg2, wu2, wd2 = _ffn_weights(ffn2_w_gate[0], ffn2_w_up[0], ffn2_w_down[0])
    cos_t, sin_t = _rope_tables(S)
    params = pltpu.CompilerParams(dimension_semantics=("arbitrary",), vmem_limit_bytes=VMEM_LIMIT_BYTES)

    tok = lambda width: pl.BlockSpec((tm, width), lambda t: (t, 0))
    tab = pl.BlockSpec((tm, LANES), lambda t: (t % tiles_per_seq, 0))
    ffn_specs = [_resident((D, D_FF)), _resident((D, D_FF)), _resident((D_FF, D))]

    kt_spec = pl.BlockSpec((1, n_heads, 1, LANES, tk),
                           lambda t: (t // tiles_per_seq, 0, t % tiles_per_seq, 0, 0))
    x1, qz, kt, v, pool = pl.pallas_call(
        functools.partial(_pre_kernel, tiles_per_seq),
        grid=(n_tiles,),
        in_specs=[tok(D), _resident((1, D))] + ffn_specs
                 + [_resident((1, D)), _resident((D, IN_WIDTH)), tab, tab,
                    _resident((len(POOL_WINDOWS), POOL_GROUP_DIM, POOL_GROUP_DIM)),
                    _resident((1, POOL_WIDTH))],
        out_specs=[tok(D), pl.BlockSpec((2, tm, ATT_WIDTH), lambda t: (0, t, 0)), kt_spec,
                   tok(ATT_WIDTH), tok(POOL_WIDTH)],
        out_shape=[jax.ShapeDtypeStruct((T, D), f32),
                   jax.ShapeDtypeStruct((2, T, ATT_WIDTH), bf16),
                   jax.ShapeDtypeStruct((B, n_heads, n_kblocks, LANES, tk), bf16),
                   jax.ShapeDtypeStruct((T, ATT_WIDTH), bf16),
                   jax.ShapeDtypeStruct((T, POOL_WIDTH), bf16)],
        scratch_shapes=[pltpu.VMEM((tm, D), f32),
                        pltpu.VMEM((POOL_HISTORY + tm, POOL_WIDTH), f32)],
        compiler_params=params,
        name="pre_ffn_proj_pool",
    )(xf, row(ffn1_norm[0]), wg1, wu1, wd1, row(mix_norm[0]), w_in[0].astype(bf16), cos_t, sin_t,
      pool_w[0].astype(bf16), row(pool_scale[0]))

    lam_spec = pl.BlockSpec((1, DIFF_HEAD_DIM), lambda b, h, g, *_: (0, 0))
    att = pl.pallas_call(
        functools.partial(_attn_kernel, gsz, n_tab),
        grid_spec=pltpu.PrefetchScalarGridSpec(
            num_scalar_prefetch=4,
            grid=(B, n_heads, n_groups),
            in_specs=[pl.BlockSpec((2, 1, gsz * tq, LANES), lambda b, h, g, *_: (0, b, g, h)),
                      pl.BlockSpec((1, 1, n_kblocks, LANES, tk), lambda b, h, g, *_: (b, h, 0, 0, 0)),
                      pl.BlockSpec((1, S, LANES), lambda b, h, g, *_: (b, 0, h)),
                      lam_spec, lam_spec, lam_spec, lam_spec,
                      pl.BlockSpec((1, DIFF_V_DIM), lambda b, h, g, *_: (0, 0))],
            out_specs=pl.BlockSpec((1, gsz * tq, LANES), lambda b, h, g, *_: (b, g, h)),
            scratch_shapes=[pltpu.VMEM((2 * tq, 2 * tk), f32), pltpu.VMEM((2 * tq, 2 * tk), f32),
                            pltpu.VMEM((2 * tq, LANES), f32), pltpu.VMEM((2 * tq, LANES), f32),
                            pltpu.VMEM((gsz * tq, LANES), f32), pltpu.VMEM((gsz * tq, LANES), f32),
                            pltpu.VMEM((gsz * tq, 2 * LANES), f32),
                            pltpu.VMEM((gsz * tq, 2 * LANES), f32),
                            pltpu.VMEM((tq, tk), f32)]),
        out_shape=jax.ShapeDtypeStruct((B, S, ATT_WIDTH), bf16),
        compiler_params=pltpu.CompilerParams(
            dimension_semantics=("arbitrary", "arbitrary", "arbitrary"),
            vmem_limit_bytes=VMEM_LIMIT_BYTES),
        name="diff_flash_attention",
    )(jnp.asarray(slot_tab), jnp.asarray(kba_tab), jnp.asarray(kbb_tab), jnp.asarray(n_pairs_tab),
      qz.reshape(2, B, S, ATT_WIDTH), kt, v.reshape(B, S, ATT_WIDTH),
      row(lambda_q1[0]), row(lambda_k1[0]), row(lambda_q2[0]), row(lambda_k2[0]), row(subln_gain[0]))

    post_rows = POST_TILES_PER_STEP * tm
    assert T % post_rows == 0
    tok2 = lambda width: pl.BlockSpec((post_rows, width), lambda t: (t, 0))
    y = pl.pallas_call(
        functools.partial(_post_kernel, tm),
        grid=(T // post_rows,),
        in_specs=[tok2(D), tok2(ATT_WIDTH), tok2(POOL_WIDTH), _resident((D, D)), _resident((1, D))]
                 + ffn_specs + [_resident((1, D))],
        out_specs=tok2(D),
        out_shape=jax.ShapeDtypeStruct((T, D), f32),
        scratch_shapes=[pltpu.VMEM((post_rows, D), f32)],
        compiler_params=params,
        name="post_proj_ffn_norm",
    )(x1, att.reshape(T, ATT_WIDTH), pool, w_out[0].astype(bf16), row(ffn2_norm[0]),
      wg2, wu2, wd2, row(final_norm))
    return y.reshape(B, S, D)
```
